```python
import math
import jax, jax.numpy as jnp
from jax import lax
import numpy as np

D_MODEL = 1024
BATCH = 1
SEQ = 16384
DEPTH = 2

GRID_W = 64
CTX_LEN = 256
EPS = 1e-6
F32 = jnp.float32

W_SSM = 256
SSM_GROUP = 16
SSM_GROUPS = W_SSM // SSM_GROUP
SSM_STATE = 64
W_LRU = 256
LRU_BLOCKS = 8
LRU_BLOCK = W_LRU // LRU_BLOCKS
LRU_CONV = 4
LRU_PAD = (2, 1)
LRU_C = 8.0
DA_HEADS = 4
DA_HEAD = 32
DA_VDIM = 2 * DA_HEAD
W_DA = DA_HEADS * DA_VDIM
ROPE_BASE = 10000.0
Q_BLOCK = 128
LAMBDA_INIT_BASE = 0.8
LAMBDA_INIT_AMP = 0.6
LAMBDA_INIT_RATE = 0.3
GLA_HEADS = 4
GLA_DK = 32
GLA_DV = 64
W_GLA = GLA_HEADS * GLA_DV
GLA_RANK = 16
GLA_TAU = 16.0
GLA_CHUNK = 64
N_BRANCH = 4
W_BRANCH = 256
D_FF = 2816
FFN_CONV = 3
FFN_PAD = (1, 1)

IN_LAYOUT = (
    ('ssm_u', W_SSM),
    ('lru_x', W_LRU), ('lru_y', W_LRU),
    ('da_q', DA_HEADS * 2 * DA_HEAD), ('da_k', DA_HEADS * 2 * DA_HEAD), ('da_v', W_DA),
    ('gla_q', GLA_HEADS * GLA_DK), ('gla_k', GLA_HEADS * GLA_DK), ('gla_v', W_GLA), ('gla_g', W_GLA),
    ('gla_a', 2 * GLA_RANK),
    ('gates', N_BRANCH * D_MODEL),
)
IN_TOTAL = sum(s for _, s in IN_LAYOUT)
ALL_PIECES = tuple(n for n, _ in IN_LAYOUT)
CTX_STATE_PIECES = ('ssm_u', 'lru_x', 'da_k', 'da_v', 'gla_k', 'gla_v', 'gla_a')

kernel_name = 'hybrid_gated_s5_rglru_diffattn_gla_dit'


def _rmsnorm(x, g):
    xf = x.astype(F32)
    y = xf * lax.rsqrt(jnp.mean(xf * xf, axis=-1, keepdims=True) + EPS)
    return (y * g.astype(F32)).astype(x.dtype)


def _adaln(cond, w, b, n_chunks):
    m = jax.nn.silu(cond) @ w[:, :n_chunks * D_MODEL] + b[:n_chunks * D_MODEL]
    return [t[:, None, :] for t in jnp.split(m, n_chunks, axis=-1)]


def _modulate(h, shift, scale):
    return h * (1.0 + scale) + shift


def _flip(t, rev):
    return t[:, ::-1] if (rev and t is not None) else t


def _dwconv(x, w, b, pad):
    y = lax.conv_general_dilated(x, w[:, None, :].astype(x.dtype), window_strides=(1,),
                                 padding=[pad], dimension_numbers=('NWC', 'WIO', 'NWC'),
                                 feature_group_count=x.shape[-1])
    return y + b.astype(x.dtype)


def _project(h, w_in, names):
    offs, o = {}, 0
    for n, s in IN_LAYOUT:
        offs[n] = (o, s)
        o += s
    w = w_in if names == ALL_PIECES else jnp.concatenate(
        [w_in[:, offs[n][0]:offs[n][0] + offs[n][1]] for n in names], axis=1)
    z = h @ w
    sizes = [offs[n][1] for n in names]
    parts = jnp.split(z, list(np.cumsum(sizes)[:-1]), axis=-1)
    return dict(zip(names, parts))


def _axial_rope(rows, dim):
    row = jnp.repeat(jnp.arange(rows), GRID_W).astype(F32)
    col = jnp.tile(jnp.arange(GRID_W), rows).astype(F32)
    half = dim // 2
    inv = 1.0 / (ROPE_BASE ** (jnp.arange(0, half, 2, dtype=F32) / half))
    ang = jnp.concatenate([row[:, None] * inv, col[:, None] * inv], axis=-1)
    return jnp.cos(ang), jnp.sin(ang)


def _apply_axial_rope(x, cos, sin):
    q = x.shape[-1] // 4
    xr = x.reshape(x.shape[:-1] + (2, 2, q))
    x1, x2 = xr[..., 0, :], xr[..., 1, :]
    c = cos.reshape(cos.shape[0], 2, q)
    s = sin.reshape(sin.shape[0], 2, q)
    out = jnp.stack([x1 * c - x2 * s, x1 * s + x2 * c], axis=-2)
    return out.reshape(x.shape).astype(x.dtype)


def _s5_discretize(lam_re, lam_im, log_step, b_re, b_im):
    lam_re, lam_im = lam_re.astype(F32), lam_im.astype(F32)
    dt = jnp.exp(log_step.astype(F32))[:, None]
    mag = jnp.exp(lam_re * dt)
    lb_re, lb_im = mag * jnp.cos(lam_im * dt), mag * jnp.sin(lam_im * dt)
    num_re, num_im = lb_re - 1.0, lb_im
    den = lam_re * lam_re + lam_im * lam_im
    k_re = (num_re * lam_re + num_im * lam_im) / den
    k_im = (num_im * lam_re - num_re * lam_im) / den
    b_re, b_im = b_re.astype(F32), b_im.astype(F32)
    bb_re = k_re[..., None] * b_re - k_im[..., None] * b_im
    bb_im = k_re[..., None] * b_im + k_im[..., None] * b_re
    return lb_re, lb_im, bb_re, bb_im


def _complex_affine_combine(e1, e2):
    a1r, a1i, b1r, b1i = e1
    a2r, a2i, b2r, b2i = e2
    return (a2r * a1r - a2i * a1i, a2r * a1i + a2i * a1r,
            a2r * b1r - a2i * b1i + b2r, a2r * b1i + a2i * b1r + b2i)


def _s5_scan(u, lb_re, lb_im, bb_re, bb_im, h0_re, h0_im):
    bu_re = jnp.einsum('blgc,gpc->blgp', u, bb_re)
    bu_im = jnp.einsum('blgc,gpc->blgp', u, bb_im)
    bu_re = bu_re.at[:, 0].add(lb_re * h0_re - lb_im * h0_im)
    bu_im = bu_im.at[:, 0].add(lb_re * h0_im + lb_im * h0_re)
    a_re = jnp.broadcast_to(lb_re, bu_re.shape)
    a_im = jnp.broadcast_to(lb_im, bu_im.shape)
    _, _, h_re, h_im = lax.associative_scan(_complex_affine_combine, (a_re, a_im, bu_re, bu_im), axis=1)
    return h_re, h_im


def _s5_readout(h_re, h_im, c_re, c_im):
    y = (jnp.einsum('blgp,gcp->blgc', h_re, c_re.astype(F32))
         - jnp.einsum('blgp,gcp->blgc', h_im, c_im.astype(F32)))
    return y.reshape(y.shape[0], y.shape[1], W_SSM)


def _s5_mixer(u_c, u_l, lam_re, lam_im, log_step, b_re, b_im, c_re, c_im, d_skip, w_glu, with_ctx):
    bn, dt = u_l.shape[0], u_l.dtype
    grp = lambda u: u.astype(F32).reshape(bn, u.shape[1], SSM_GROUPS, SSM_GROUP)
    g_c, g_l = grp(u_c), grp(u_l)
    d32 = d_skip.astype(F32)
    y_l = u_l.astype(F32) * d32
    y_c = u_c.astype(F32) * d32 if with_ctx else None
    zero = jnp.zeros((bn, SSM_GROUPS, SSM_STATE), F32)
    for di in range(2):
        rev = di == 1
        lb_re, lb_im, bb_re, bb_im = _s5_discretize(lam_re[di], lam_im[di], log_step[di], b_re[di], b_im[di])
        hc_re, hc_im = _s5_scan(_flip(g_c, rev), lb_re, lb_im, bb_re, bb_im, zero, zero)
        hl_re, hl_im = _s5_scan(_flip(g_l, rev), lb_re, lb_im, bb_re, bb_im, hc_re[:, -1], hc_im[:, -1])
        y_l = y_l + _flip(_s5_readout(hl_re, hl_im, c_re[di], c_im[di]), rev)
        if with_ctx:
            y_c = y_c + _flip(_s5_readout(hc_re, hc_im, c_re[di], c_im[di]), rev)

    def glu(y):
        z = jax.nn.gelu(y)
        return (z * jax.nn.sigmoid(z @ w_glu.astype(F32))).astype(dt)
    return (glu(y_c) if with_ctx else None), glu(y_l)


def _rglru_gates(xc, wr, br, wi, bi, lam):
    bn, n, _ = xc.shape
    blk = xc.reshape(bn, n, LRU_BLOCKS, LRU_BLOCK)
    r = jax.nn.sigmoid(jnp.einsum('blnc,ncd->blnd', blk, wr.astype(F32)).reshape(bn, n, W_LRU) + br.astype(F32))
    i = jax.nn.sigmoid(jnp.einsum('blnc,ncd->blnd', blk, wi.astype(F32)).reshape(bn, n, W_LRU) + bi.astype(F32))
    log_a = LRU_C * r * jax.nn.log_sigmoid(lam.astype(F32))
    a = jnp.exp(log_a)
    b = jnp.sqrt(-jnp.expm1(2.0 * log_a)) * (i * xc)
    return a, b


def _linear_scan(a, b, h0):
    b = b.at[:, 0].add(a[:, 0] * h0)
    _, h = lax.associative_scan(lambda e1, e2: (e1[0] * e2[0], e2[0] * e1[1] + e2[1]), (a, b), axis=1)
    return h


def _rglru_mixer(z_c, z_l, conv_w, conv_b, wr, br, wi, bi, lam, with_ctx):
    bn, dt = z_l['lru_x'].shape[0], z_l['lru_x'].dtype
    xc_l = _dwconv(z_l['lru_x'], conv_w, conv_b, LRU_PAD).astype(F32)
    xc_c = _dwconv(z_c['lru_x'], conv_w, conv_b, LRU_PAD).astype(F32)
    zero = jnp.zeros((bn, W_LRU), F32)
    h_l, h_c = 0.0, 0.0
    for di in range(2):
        rev = di == 1
        a_c, b_c = _rglru_gates(_flip(xc_c, rev), wr[di], br[di], wi[di], bi[di], lam[di])
        hs_c = _linear_scan(a_c, b_c, zero)
        a_l, b_l = _rglru_gates(_flip(xc_l, rev), wr[di], br[di], wi[di], bi[di], lam[di])
        hs_l = _linear_scan(a_l, b_l, hs_c[:, -1])
        h_l = h_l + _flip(hs_l, rev)
        if with_ctx:
            h_c = h_c + _flip(hs_c, rev)
    y_l = (h_l * jax.nn.gelu(z_l['lru_y'].astype(F32))).astype(dt)
    y_c = (h_c * jax.nn.gelu(z_c['lru_y'].astype(F32))).astype(dt) if with_ctx else None
    return y_c, y_l


def _diff_softmax(q, k, v, lam):
    s = jnp.einsum('bhcqd,bhckd->bhcqk', q, k).astype(F32) * (DA_HEAD ** -0.5)
    p = jax.nn.softmax(s, axis=-1)
    w = p[:, :, 0] - lam * p[:, :, 1]
    return jnp.einsum('bhqk,bhkv->bhqv', w.astype(v.dtype), v)


def _diff_attention(z_c, z_l, q_norm, k_norm, lam_p, out_norm, cos, sin, lam_init, with_ctx):
    bn, n_lat, _ = z_l['da_q'].shape

    def heads_qk(t, g):
        t = _rmsnorm(t.reshape(bn, t.shape[1], DA_HEADS, 2, DA_HEAD), g)
        return t.transpose(0, 2, 3, 1, 4)

    def heads_v(t):
        return t.reshape(bn, t.shape[1], DA_HEADS, DA_VDIM).transpose(0, 2, 1, 3)

    q_l = _apply_axial_rope(heads_qk(z_l['da_q'], q_norm), cos, sin)
    k_l = _apply_axial_rope(heads_qk(z_l['da_k'], k_norm), cos, sin)
    v_l = heads_v(z_l['da_v'])
    k_c = heads_qk(z_c['da_k'], k_norm)
    v_c = heads_v(z_c['da_v'])
    lp = lam_p.astype(F32)
    lam = jnp.exp(jnp.sum(lp[0] * lp[1])) - jnp.exp(jnp.sum(lp[2] * lp[3])) + lam_init
    k_all = jnp.concatenate([k_c, k_l], axis=3)
    v_all = jnp.concatenate([v_c, v_l], axis=2)
    nb = n_lat // Q_BLOCK
    q_blocks = q_l.reshape(bn, DA_HEADS, 2, nb, Q_BLOCK, DA_HEAD).transpose(3, 0, 1, 2, 4, 5)
    o_l = lax.map(lambda qb: _diff_softmax(qb, k_all, v_all, lam), q_blocks)
    o_l = o_l.transpose(1, 2, 0, 3, 4).reshape(bn, DA_HEADS, n_lat, DA_VDIM)

    def finish(o):
        o = _rmsnorm(o, out_norm) * (1.0 - lam_init)
        return o.transpose(0, 2, 1, 3).reshape(bn, o.shape[2], W_DA)

    y_c = finish(_diff_softmax(heads_qk(z_c['da_q'], q_norm), k_c, v_c, lam)) if with_ctx else None
    return y_c, finish(o_l)


def _gla_chunked(q, k, v, log_a, s0, with_out):
    bn, n_tok, nh, _ = k.shape
    dv = v.shape[-1]
    nc = n_tok // GLA_CHUNK
    chunks = lambda t: t.reshape(bn, nc, GLA_CHUNK, nh, t.shape[-1]).transpose(1, 0, 3, 2, 4)
    kc, vc = chunks(k), chunks(v)
    b = jnp.cumsum(chunks(log_a), axis=-2)
    b_last = b[..., -1:, :]
    kv = jnp.einsum('nbhjd,nbhjv->nbhdv', kc * jnp.exp(b_last - b), vc)
    decay = jnp.exp(b_last[..., 0, :])

    def step(s, xs):
        dec, kvn = xs
        return dec[..., None] * s + kvn, (s if with_out else None)
    s_fin, s_start = lax.scan(step, s0, (decay, kv))
    if not with_out:
        return None, s_fin
    qc = chunks(q) * jnp.exp(b)
    mask = jnp.tril(jnp.ones((GLA_CHUNK, GLA_CHUNK), dtype=bool))
    att = jnp.where(mask, jnp.einsum('nbhid,nbhjd->nbhij', qc, kc * jnp.exp(-b)), 0.0)
    o = jnp.einsum('nbhij,nbhjv->nbhiv', att, vc) + jnp.einsum('nbhid,nbhdv->nbhiv', qc, s_start)
    return o.transpose(1, 0, 3, 2, 4).reshape(bn, n_tok, nh, dv), s_fin


def _gla_mixer(z_c, z_l, wa2, ba, out_norm, with_ctx):
    bn, dt = z_l['gla_q'].shape[0], z_l['gla_q'].dtype
    heads = lambda t, d: t.astype(F32).reshape(bn, t.shape[1], GLA_HEADS, d)
    q_l = heads(z_l['gla_q'], GLA_DK) * (GLA_DK ** -0.5)
    k_l, v_l = heads(z_l['gla_k'], GLA_DK), heads(z_l['gla_v'], GLA_DV)
    q_c = heads(z_c['gla_q'], GLA_DK) * (GLA_DK ** -0.5) if with_ctx else None
    k_c, v_c = heads(z_c['gla_k'], GLA_DK), heads(z_c['gla_v'], GLA_DV)

    def log_gate(a_low, di):
        z = a_low[..., di * GLA_RANK:(di + 1) * GLA_RANK].astype(F32) @ wa2[di].astype(F32) + ba[di].astype(F32)
        return heads(jax.nn.log_sigmoid(z) / GLA_TAU, GLA_DK)

    s0 = jnp.zeros((bn, GLA_HEADS, GLA_DK, GLA_DV), F32)
    o_l, o_c = 0.0, 0.0
    for di in range(2):
        rev = di == 1
        out_c, s_c = _gla_chunked(_flip(q_c, rev), _flip(k_c, rev), _flip(v_c, rev),
                                  _flip(log_gate(z_c['gla_a'], di), rev), s0, with_ctx)
        out_l, _ = _gla_chunked(_flip(q_l, rev), _flip(k_l, rev), _flip(v_l, rev),
                                _flip(log_gate(z_l['gla_a'], di), rev), s_c, True)
        o_l = o_l + _flip(out_l, rev)
        if with_ctx:
            o_c = o_c + _flip(out_c, rev)

    def finish(o, g):
        o = _rmsnorm(o, out_norm).reshape(bn, o.shape[1], W_GLA)
        return (o * jax.nn.silu(g.astype(F32))).astype(dt)
    return (finish(o_c, z_c['gla_g']) if with_ctx else None), finish(o_l, z_l['gla_g'])


def _merge(gate_logits, branches, w_branch, w_out):
    bn, n, _ = gate_logits.shape
    g = jax.nn.sigmoid(gate_logits.reshape(bn, n, N_BRANCH, D_MODEL))
    y = jnp.stack(branches, axis=2)
    proj = jnp.einsum('blnw,nwd->blnd', y, w_branch)
    return jnp.einsum('blnd,blnd->bld', g, proj) @ w_out


def _conv_ffn(h, w_up, conv_w, conv_b, w_down):
    a, gt = jnp.split(h @ w_up, 2, axis=-1)
    gt = _dwconv(gt, conv_w, conv_b, FFN_PAD)
    return (jax.nn.gelu(gt) * a) @ w_down


def _diff_lambda_init(layer):
    return LAMBDA_INIT_BASE - LAMBDA_INIT_AMP * math.exp(-LAMBDA_INIT_RATE * layer)


def setup_inputs(seed: int = 0) -> dict:
    key = jax.random.key(seed)
    ks = iter(jax.random.split(key, 40))
    nrm = lambda shape, scale: jax.random.normal(next(ks), shape, F32) * scale
    gain = lambda shape: 1.0 + nrm(shape, 0.05)
    L2 = (DEPTH, 2)
    lam_u = jax.random.uniform(next(ks), L2 + (W_LRU,), F32, 0.9, 0.999) ** (1.0 / LRU_C)
    return {
        'x': nrm((BATCH, SEQ, D_MODEL), 1.0),
        'c': nrm((BATCH, D_MODEL), 1.0),
        'ctx': nrm((BATCH, CTX_LEN, D_MODEL), 1.0),
        'c_ctx': nrm((D_MODEL,), 1.0),
        'w_ada': nrm((DEPTH, D_MODEL, 6 * D_MODEL), 0.5 * D_MODEL ** -0.5),
        'b_ada': nrm((DEPTH, 6 * D_MODEL), 0.01),
        'norm1_g': gain((DEPTH, D_MODEL)),
        'norm2_g': gain((DEPTH, D_MODEL)),
        'w_in': nrm((DEPTH, D_MODEL, IN_TOTAL), D_MODEL ** -0.5),
        'ssm_lam_re': -0.5 + nrm(L2 + (SSM_GROUPS, SSM_STATE), 0.01),
        'ssm_lam_im': jnp.pi * jnp.arange(SSM_STATE, dtype=F32) + nrm(L2 + (SSM_GROUPS, SSM_STATE), 0.01),
        'ssm_log_step': jax.random.uniform(next(ks), L2 + (SSM_GROUPS,), F32, math.log(1e-3), math.log(1e-1)),
        'ssm_b_re': nrm(L2 + (SSM_GROUPS, SSM_STATE, SSM_GROUP), (2 * SSM_GROUP) ** -0.5),
        'ssm_b_im': nrm(L2 + (SSM_GROUPS, SSM_STATE, SSM_GROUP), (2 * SSM_GROUP) ** -0.5),
        'ssm_c_re': nrm(L2 + (SSM_GROUPS, SSM_GROUP, SSM_STATE), SSM_STATE ** -0.5),
        'ssm_c_im': nrm(L2 + (SSM_GROUPS, SSM_GROUP, SSM_STATE), SSM_STATE ** -0.5),
        'ssm_d': nrm((DEPTH, W_SSM), 0.5),
        'ssm_w_glu': nrm((DEPTH, W_SSM, W_SSM), W_SSM ** -0.5),
        'lru_conv_w': nrm((DEPTH, LRU_CONV, W_LRU), LRU_CONV ** -0.5),
        'lru_conv_b': nrm((DEPTH, W_LRU), 0.01),
        'lru_wr': nrm(L2 + (LRU_BLOCKS, LRU_BLOCK, LRU_BLOCK), LRU_BLOCK ** -0.5),
        'lru_br': nrm(L2 + (W_LRU,), 0.01),
        'lru_wi': nrm(L2 + (LRU_BLOCKS, LRU_BLOCK, LRU_BLOCK), LRU_BLOCK ** -0.5),
        'lru_bi': nrm(L2 + (W_LRU,), 0.01),
        'lru_lam': jnp.log(lam_u) - jnp.log1p(-lam_u),
        'da_q_norm': gain((DEPTH, DA_HEAD)),
        'da_k_norm': gain((DEPTH, DA_HEAD)),
        'da_lam': nrm((DEPTH, 4, DA_HEAD), 0.1),
        'da_out_norm': gain((DEPTH, DA_VDIM)),
        'gla_wa2': nrm(L2 + (GLA_RANK, GLA_HEADS * GLA_DK), GLA_RANK ** -0.5),
        'gla_ba': nrm(L2 + (GLA_HEADS * GLA_DK,), 0.1),
        'gla_out_norm': gain((DEPTH, GLA_DV)),
        'w_branch': nrm((DEPTH, N_BRANCH, W_BRANCH, D_MODEL), W_BRANCH ** -0.5),
        'w_out': nrm((DEPTH, D_MODEL, D_MODEL), D_MODEL ** -0.5),
        'w_up': nrm((DEPTH, D_MODEL, 2 * D_FF), D_MODEL ** -0.5),
        'ffn_conv_w': nrm((DEPTH, FFN_CONV, D_FF), FFN_CONV ** -0.5),
        'ffn_conv_b': nrm((DEPTH, D_FF), 0.01),
        'w_down': nrm((DEPTH, D_FF, D_MODEL), D_FF ** -0.5),
    }


def reference(x, c, ctx, c_ctx, w_ada, b_ada, norm1_g, norm2_g, w_in,
              ssm_lam_re, ssm_lam_im, ssm_log_step, ssm_b_re, ssm_b_im, ssm_c_re, ssm_c_im, ssm_d, ssm_w_glu,
              lru_conv_w, lru_conv_b, lru_wr, lru_br, lru_wi, lru_bi, lru_lam,
              da_q_norm, da_k_norm, da_lam, da_out_norm,
              gla_wa2, gla_ba, gla_out_norm,
              w_branch, w_out, w_up, ffn_conv_w, ffn_conv_b, w_down):
    rows = x.shape[1] // GRID_W
    cos, sin = _axial_rope(rows, DA_HEAD)
    h_ctx = ctx
    for l in range(DEPTH):
        with_ctx = l < DEPTH - 1
        sh1, sc1, g1, sh2, sc2, g2 = _adaln(c, w_ada[l], b_ada[l], 6)
        cmod = _adaln(c_ctx[None], w_ada[l], b_ada[l], 6 if with_ctx else 2)
        hn_l = _modulate(_rmsnorm(x, norm1_g[l]), sh1, sc1)
        hn_c = _modulate(_rmsnorm(h_ctx, norm1_g[l]), cmod[0], cmod[1])
        z_l = _project(hn_l, w_in[l], ALL_PIECES)
        z_c = _project(hn_c, w_in[l], ALL_PIECES if with_ctx else CTX_STATE_PIECES)
        ya_c, ya_l = _s5_mixer(z_c['ssm_u'], z_l['ssm_u'], ssm_lam_re[l], ssm_lam_im[l], ssm_log_step[l],
                               ssm_b_re[l], ssm_b_im[l], ssm_c_re[l], ssm_c_im[l], ssm_d[l], ssm_w_glu[l], with_ctx)
        yb_c, yb_l = _rglru_mixer(z_c, z_l, lru_conv_w[l], lru_conv_b[l], lru_wr[l], lru_br[l],
                                  lru_wi[l], lru_bi[l], lru_lam[l], with_ctx)
        yc_c, yc_l = _diff_attention(z_c, z_l, da_q_norm[l], da_k_norm[l], da_lam[l], da_out_norm[l],
                                     cos, sin, _diff_lambda_init(l), with_ctx)
        yd_c, yd_l = _gla_mixer(z_c, z_l, gla_wa2[l], gla_ba[l], gla_out_norm[l], with_ctx)
        x = x + g1 * _merge(z_l['gates'], (ya_l, yb_l, yc_l, yd_l), w_branch[l], w_out[l])
        x = x + g2 * _conv_ffn(_modulate(_rmsnorm(x, norm2_g[l]), sh2, sc2),
                               w_up[l], ffn_conv_w[l], ffn_conv_b[l], w_down[l])
        if with_ctx:
            h_ctx = h_ctx + cmod[2] * _merge(z_c['gates'], (ya_c, yb_c, yc_c, yd_c), w_branch[l], w_out[l])
            h_ctx = h_ctx + cmod[5] * _conv_ffn(_modulate(_rmsnorm(h_ctx, norm2_g[l]), cmod[3], cmod[4]),
                                                w_up[l], ffn_conv_w[l], ffn_conv_b[l], w_down[l])
    return x
```

```python
import functools
import math

import jax
import jax.numpy as jnp
from jax import lax
from jax.experimental import pallas as pl
from jax.experimental.pallas import tpu as pltpu

F32 = jnp.float32
BF16 = jnp.bfloat16
HIGHEST = lax.Precision.HIGHEST

D_MODEL = 1024
EPS = 1e-6
GRID_W = 64
TILE = 256
SUBLANES = 8
HALO = SUBLANES
VMEM_LIMIT = 56 * 1024 * 1024

SSM_GROUPS, SSM_GROUP, SSM_STATE = 16, 16, 64
W_SSM = 256
N_STATE = SSM_GROUPS * SSM_STATE
W_LRU, LRU_BLOCKS, LRU_BLOCK, LRU_C = 256, 8, 32, 8.0
DA_HEADS, DA_HEAD, DA_VDIM = 4, 32, 64
N_PAIR = 2 * DA_HEADS
ROPE_BASE = 10000.0
GLA_HEADS, GLA_DK, GLA_DV, GLA_RANK, GLA_TAU, GLA_CHUNK = 4, 32, 64, 16, 16.0, 64
W_GLA_K = GLA_HEADS * GLA_DK
N_BRANCH, W_BRANCH, D_FF = 4, 256, 2816
W_MAIN = 2304
W_GATES = N_BRANCH * D_MODEL


def _dot(a, b, precision=None):
    return jnp.dot(a, b, preferred_element_type=F32, precision=precision)


def _dot_nt(a, b, precision=None):
    return lax.dot_general(a, b, (((1,), (1,)), ((), ())), preferred_element_type=F32, precision=precision)


def _params(*sem):
    return pltpu.CompilerParams(dimension_semantics=sem, vmem_limit_bytes=VMEM_LIMIT)


def _const_spec(shape):
    nd = len(shape)
    return pl.BlockSpec(shape, lambda *_: (0,) * nd)


def _rms(x, g):
    return x * lax.rsqrt(jnp.mean(x * x, axis=-1, keepdims=True) + EPS) * g


def _adaln_kernel(c_ref, w_ref, b_ref, o_ref):
    c = c_ref[...]
    o_ref[0] = _dot(c * jax.nn.sigmoid(c), w_ref[0], HIGHEST) + b_ref[0]


def _adaln(cond, w_ada, b_ada):
    depth, _, n_out = w_ada.shape
    blk = 1536
    return pl.pallas_call(
        _adaln_kernel,
        grid=(depth, n_out // blk),
        in_specs=[pl.BlockSpec((SUBLANES, D_MODEL), lambda l, n: (0, 0)),
                  pl.BlockSpec((1, D_MODEL, blk), lambda l, n: (l, 0, n)),
                  pl.BlockSpec((1, 1, blk), lambda l, n: (l, 0, n))],
        out_specs=pl.BlockSpec((1, SUBLANES, blk), lambda l, n: (l, 0, n)),
        out_shape=jax.ShapeDtypeStruct((depth, SUBLANES, n_out), F32),
        compiler_params=_params("arbitrary", "arbitrary"),
        name="adaln",
    )(cond, w_ada, b_ada.reshape(depth, 1, n_out))


def _inproj_kernel(x_ref, mod_ref, g_ref, wm_ref, wa_ref, wg_ref, cos_ref, sin_ref, qg_ref, kg_ref, grp_ref,
                   u_ref, lx_ref, ly_ref, q_ref, kt_ref, v_ref, gqk_ref, gv_ref, gg_ref, ga_ref, gates_ref):
    x = x_ref[...]
    mod = mod_ref[0]
    hn = _rms(x, g_ref[...]) * (1.0 + mod[1:2]) + mod[0:1]
    hb = hn.astype(BF16)
    z = _dot(hb, wm_ref[...])
    u_ref[...] = z[:, 0:256]
    lx_ref[...] = z[:, 256:512]
    ly_ref[...] = z[:, 512:768]
    gqk_ref[...] = z[:, 1536:1792]
    gv_ref[...] = z[:, 1792:2048]
    gg_ref[...] = z[:, 2048:2304]
    ga_ref[...] = _dot(hb, wa_ref[...])
    gates_ref[...] = _dot(hb, wg_ref[...])

    cos, sin = cos_ref[...], sin_ref[...]
    lane = lax.broadcasted_iota(jnp.int32, (TILE, 256), 1)
    first_half = (lane % 16) < 8
    grp = grp_ref[...]

    def qk_norm_rope(t, g):
        ms = _dot(t * t, grp, HIGHEST)
        tn = t * lax.rsqrt(ms + EPS) * g
        partner = jnp.where(first_half, pltpu.roll(tn, 256 - 8, 1), pltpu.roll(tn, 8, 1))
        return tn * cos + partner * sin

    q_ref[...] = (qk_norm_rope(z[:, 768:1024], qg_ref[...]) * (DA_HEAD ** -0.5)).astype(BF16)
    kt_ref[...] = qk_norm_rope(z[:, 1024:1280], kg_ref[...]).T.astype(BF16)
    v = z[:, 1280:1536].astype(BF16)
    for h in range(DA_HEADS):
        v_ref[h] = v[:, h * DA_VDIM:(h + 1) * DA_VDIM]


def _inproj(xa, mods, norm_g, w_main, w_a, w_gates, cos_t, sin_t, qg, kg, grp32):
    t_all = xa.shape[0]
    nt = t_all // TILE
    row = lambda w: pl.BlockSpec((TILE, w), lambda i: (i, 0))
    f = lambda w: jax.ShapeDtypeStruct((t_all, w), F32)
    return pl.pallas_call(
        _inproj_kernel,
        grid=(nt,),
        in_specs=[row(D_MODEL),
                  pl.BlockSpec((1, SUBLANES, D_MODEL), lambda i: (jnp.minimum(i, 1), 0, 0)),
                  _const_spec((1, D_MODEL)),
                  _const_spec((D_MODEL, W_MAIN)), _const_spec((D_MODEL, 128)), _const_spec((D_MODEL, W_GATES)),
                  row(256), row(256), _const_spec((1, 256)), _const_spec((1, 256)), _const_spec((256, 256))],
        out_specs=[row(256), row(256), row(256), row(256),
                   pl.BlockSpec((256, TILE), lambda i: (0, i)),
                   pl.BlockSpec((DA_HEADS, TILE, DA_VDIM), lambda i: (0, i, 0)),
                   row(256), row(256), row(256), row(128), row(W_GATES)],
        out_shape=[f(256), f(256), f(256), jax.ShapeDtypeStruct((t_all, 256), BF16),
                   jax.ShapeDtypeStruct((256, t_all), BF16),
                   jax.ShapeDtypeStruct((DA_HEADS, t_all, DA_VDIM), BF16),
                   f(256), f(256), f(256), f(128), f(W_GATES)],
        compiler_params=_params("arbitrary"),
        name="inproj",
    )(xa, mods, norm_g, w_main, w_a, w_gates, cos_t, sin_t, qg, kg, grp32)


def _scan_tile_index(j, nt, rev):
    return jnp.where(j == 0, 0, nt - j) if rev else j


def _s5_kernel(u_ref, b_ref, c_ref, tab_ref, y_ref, hre, him, car, *, rev):
    j = pl.program_id(0)

    @pl.when(j == 0)
    def _():
        car[...] = jnp.zeros_like(car)

    bu = _dot(u_ref[...].astype(BF16), b_ref[...])
    hre[...] = bu[:, :N_STATE]
    him[...] = bu[:, N_STATE:]
    nblk = TILE // SUBLANES

    def body(b, carry):
        cr, ci = carry
        blk = nblk - 1 - b if rev else b
        r0 = pl.multiple_of(blk * SUBLANES, SUBLANES)
        xr = hre[pl.ds(r0, SUBLANES), :]
        xi = him[pl.ds(r0, SUBLANES), :]
        for k, s in enumerate((1, 2, 4)):
            shift = SUBLANES - s if rev else s
            sr = pltpu.roll(xr, shift, 0)
            si = pltpu.roll(xi, shift, 0)
            mr, mi = tab_ref[2 * k], tab_ref[2 * k + 1]
            xr, xi = xr + mr * sr - mi * si, xi + mr * si + mi * sr
        pr, pi_ = tab_ref[6], tab_ref[7]
        xr, xi = xr + pr * cr - pi_ * ci, xi + pr * ci + pi_ * cr
        hre[pl.ds(r0, SUBLANES), :] = xr
        him[pl.ds(r0, SUBLANES), :] = xi
        last = 0 if rev else SUBLANES - 1
        return (jnp.broadcast_to(xr[last:last + 1], xr.shape), jnp.broadcast_to(xi[last:last + 1], xi.shape))

    cr, ci = lax.fori_loop(0, nblk, body, (car[0], car[1]))
    car[0] = cr
    car[1] = ci
    y_ref[...] = (_dot(hre[...].astype(BF16), c_ref[:N_STATE]) + _dot(him[...].astype(BF16), c_ref[N_STATE:]))


def _s5_scan(u, bcat, ccat, tabs, rev):
    t_all = u.shape[0]
    nt = t_all // TILE
    tix = lambda j: (_scan_tile_index(j, nt, rev), 0)
    return pl.pallas_call(
        functools.partial(_s5_kernel, rev=rev),
        grid=(nt,),
        in_specs=[pl.BlockSpec((TILE, W_SSM), tix), _const_spec((W_SSM, 2 * N_STATE)),
                  _const_spec((2 * N_STATE, W_SSM)), _const_spec((8, SUBLANES, N_STATE))],
        out_specs=pl.BlockSpec((TILE, W_SSM), tix),
        out_shape=jax.ShapeDtypeStruct((t_all, W_SSM), F32),
        scratch_shapes=[pltpu.VMEM((TILE, N_STATE), F32), pltpu.VMEM((TILE, N_STATE), F32),
                        pltpu.VMEM((2, SUBLANES, N_STATE), F32)],
        compiler_params=_params("arbitrary"),
        name="s5_rev" if rev else "s5_fwd",
    )(u, bcat, ccat, tabs)


def _s5_tables(lam_re, lam_im, log_step, b_re, b_im, c_re, c_im, rev):
    dt = jnp.exp(log_step.astype(F32))[:, None]
    lre, lim = lam_re.astype(F32) * dt, lam_im.astype(F32) * dt
    mag = jnp.exp(lre)
    lb_re, lb_im = mag * jnp.cos(lim), mag * jnp.sin(lim)
    num_re, num_im = lb_re - 1.0, lb_im
    den = lam_re * lam_re + lam_im * lam_im
    k_re = (num_re * lam_re + num_im * lam_im) / den
    k_im = (num_im * lam_re - num_re * lam_im) / den
    bb_re = k_re[..., None] * b_re - k_im[..., None] * b_im
    bb_im = k_re[..., None] * b_im + k_im[..., None] * b_re
    eye = jnp.eye(SSM_GROUPS, dtype=F32)
    blk_in = lambda t: jnp.einsum('gpc,gh->gchp', t, eye).reshape(W_SSM, N_STATE)
    blk_out = lambda t: jnp.einsum('gcp,gh->gphc', t, eye).reshape(N_STATE, W_SSM)
    bcat = jnp.concatenate([blk_in(bb_re), blk_in(bb_im)], axis=1).astype(BF16)
    ccat = jnp.concatenate([blk_out(c_re.astype(F32)), -blk_out(c_im.astype(F32))], axis=0).astype(BF16)

    def power(k):
        k = k[:, None].astype(F32)
        m = jnp.exp(k * lre.reshape(1, -1))
        return m * jnp.cos(k * lim.reshape(1, -1)), m * jnp.sin(k * lim.reshape(1, -1))

    r = jnp.arange(SUBLANES)
    tabs = []
    for s in (1, 2, 4):
        pr, pi_ = power(jnp.full((SUBLANES,), s))
        keep = ((r + s <= SUBLANES - 1) if rev else (r >= s))[:, None]
        tabs += [jnp.where(keep, pr, 0.0), jnp.where(keep, pi_, 0.0)]
    tabs += list(power((SUBLANES - r) if rev else (r + 1)))
    return bcat, ccat, jnp.stack(tabs)


def _lru_kernel(x_ref, xp_ref, xn_ref, cw_ref, cb_ref, wr_ref, br_ref, wi_ref, bi_ref, lam_ref,
                h_ref, ext, a_sc, b_sc, car, *, rev, nt):
    j = pl.program_id(0)
    ti = _scan_tile_index(j, nt, rev)

    @pl.when(j == 0)
    def _():
        car[...] = jnp.zeros_like(car)

    prev_ok = (ti >= 2).astype(F32)
    next_ok = jnp.logical_and(ti != 0, ti != nt - 1).astype(F32)
    ext[0:HALO] = xp_ref[...] * prev_ok
    ext[HALO:HALO + TILE] = x_ref[...]
    ext[HALO + TILE:] = xn_ref[...] * next_ok
    xc = cb_ref[...] + sum(cw_ref[k:k + 1] * ext[pl.ds(HALO - 2 + k, TILE), :] for k in range(4))
    r = jax.nn.sigmoid(_dot(xc, wr_ref[...], HIGHEST) + br_ref[...])
    i = jax.nn.sigmoid(_dot(xc, wi_ref[...], HIGHEST) + bi_ref[...])
    log_a = LRU_C * r * jax.nn.log_sigmoid(lam_ref[...])
    a = jnp.exp(log_a)
    a_sc[...] = a
    b_sc[...] = jnp.sqrt(-jnp.tanh(log_a) * (a * a + 1.0)) * (i * xc)
    nblk = TILE // SUBLANES
    rows = lax.broadcasted_iota(jnp.int32, (SUBLANES, W_LRU), 0)

    def body(b, carry):
        blk = nblk - 1 - b if rev else b
        r0 = pl.multiple_of(blk * SUBLANES, SUBLANES)
        a = a_sc[pl.ds(r0, SUBLANES), :]
        x = b_sc[pl.ds(r0, SUBLANES), :]
        for s in (1, 2, 4):
            shift = SUBLANES - s if rev else s
            keep = (rows + s <= SUBLANES - 1) if rev else (rows >= s)
            a_s = jnp.where(keep, pltpu.roll(a, shift, 0), 1.0)
            x_s = jnp.where(keep, pltpu.roll(x, shift, 0), 0.0)
            a, x = a * a_s, a * x_s + x
        h = x + a * carry
        b_sc[pl.ds(r0, SUBLANES), :] = h
        last = 0 if rev else SUBLANES - 1
        return jnp.broadcast_to(h[last:last + 1], h.shape)

    car[...] = lax.fori_loop(0, nblk, body, car[...])
    h_ref[...] = b_sc[...]


def _lru_scan(lx, conv_w, conv_b, wr, br, wi, bi, lam, rev):
    t_all = lx.shape[0]
    nt = t_all // TILE
    per = TILE // HALO
    tix = lambda j: (_scan_tile_index(j, nt, rev), 0)
    pix = lambda j: (jnp.maximum(_scan_tile_index(j, nt, rev) * per - 1, 0), 0)
    nix = lambda j: (jnp.minimum((_scan_tile_index(j, nt, rev) + 1) * per, nt * per - 1), 0)
    vec = _const_spec((1, W_LRU))
    return pl.pallas_call(
        functools.partial(_lru_kernel, rev=rev, nt=nt),
        grid=(nt,),
        in_specs=[pl.BlockSpec((TILE, W_LRU), tix), pl.BlockSpec((HALO, W_LRU), pix), pl.BlockSpec((HALO, W_LRU), nix),
                  _const_spec((4, W_LRU)), vec, _const_spec((W_LRU, W_LRU)), vec, _const_spec((W_LRU, W_LRU)), vec, vec],
        out_specs=pl.BlockSpec((TILE, W_LRU), tix),
        out_shape=jax.ShapeDtypeStruct((t_all, W_LRU), F32),
        scratch_shapes=[pltpu.VMEM((TILE + 2 * HALO, W_LRU), F32), pltpu.VMEM((TILE, W_LRU), F32),
                        pltpu.VMEM((TILE, W_LRU), F32), pltpu.VMEM((SUBLANES, W_LRU), F32)],
        compiler_params=_params("arbitrary"),
        name="lru_rev" if rev else "lru_fwd",
    )(lx, lx, lx, conv_w, conv_b, wr, br, wi, bi, lam)


def _block_diag(w):
    n, a, b = w.shape
    return jnp.einsum('nab,nm->namb', w, jnp.eye(n, dtype=w.dtype)).reshape(n * a, n * b)


def _attn_kernel(lam_ref, q_ref, kt_ref, v_ref, o_ref, m_sc, l_sc, acc_sc, *, nkv):
    j = pl.program_id(1)

    @pl.when(j == 0)
    def _():
        m_sc[...] = jnp.full_like(m_sc, -jnp.inf)
        l_sc[...] = jnp.zeros_like(l_sc)
        acc_sc[...] = jnp.zeros_like(acc_sc)

    q = q_ref[...]
    for p in range(N_PAIR):
        h = p // 2
        s = _dot(q[:, p * DA_HEAD:(p + 1) * DA_HEAD], kt_ref[p * DA_HEAD:(p + 1) * DA_HEAD, :])
        m_old = m_sc[p]
        m_new = jnp.maximum(m_old, jnp.max(s, axis=-1, keepdims=True))
        alpha = jnp.exp(m_old - m_new)
        e = jnp.exp(s - m_new)
        l_sc[p] = alpha * l_sc[p] + jnp.sum(e, axis=-1, keepdims=True)
        acc_sc[p] = alpha * acc_sc[p] + _dot(e.astype(BF16), v_ref[h])
        m_sc[p] = m_new

    @pl.when(j == nkv - 1)
    def _():
        lam = lam_ref[0, 0]
        for h in range(DA_HEADS):
            o_ref[h] = acc_sc[2 * h] / l_sc[2 * h] - lam * (acc_sc[2 * h + 1] / l_sc[2 * h + 1])


def _attention(lam, q, kt, v, tq, tk):
    nq, nkv = q.shape[0] // tq, kt.shape[1] // tk
    return pl.pallas_call(
        functools.partial(_attn_kernel, nkv=nkv),
        grid=(nq, nkv),
        in_specs=[pl.BlockSpec(memory_space=pltpu.SMEM),
                  pl.BlockSpec((tq, 256), lambda i, j: (i, 0)),
                  pl.BlockSpec((256, tk), lambda i, j: (0, j)),
                  pl.BlockSpec((DA_HEADS, tk, DA_VDIM), lambda i, j: (0, j, 0))],
        out_specs=pl.BlockSpec((DA_HEADS, tq, DA_VDIM), lambda i, j: (0, i, 0)),
        out_shape=jax.ShapeDtypeStruct((DA_HEADS, q.shape[0], DA_VDIM), F32),
        scratch_shapes=[pltpu.VMEM((N_PAIR, tq, 1), F32), pltpu.VMEM((N_PAIR, tq, 1), F32),
                        pltpu.VMEM((N_PAIR, tq, DA_VDIM), F32)],
        compiler_params=_params("arbitrary", "arbitrary"),
        name="diff_attn",
    )(lam, q, kt, v)


def _gla_kernel(qk_ref, v_ref, ga_ref, wa_ref, ba_ref, tri_ref, sel_ref, bd_ref, o_ref, s_sc, *, rev):
    j = pl.program_id(0)

    @pl.when(j == 0)
    def _():
        s_sc[...] = jnp.zeros_like(s_sc)

    q = qk_ref[:, :W_GLA_K] * (GLA_DK ** -0.5)
    k = qk_ref[:, W_GLA_K:]
    v = v_ref[...]
    lg = jax.nn.log_sigmoid(_dot(ga_ref[...], wa_ref[...], HIGHEST) + ba_ref[...]) / GLA_TAU
    b = _dot(tri_ref[...], lg, HIGHEST)
    b_last = _dot(sel_ref[...], lg, HIGHEST)
    qc = q * jnp.exp(b)
    kinv = k * jnp.exp(-b)
    kdec_t = (k * jnp.exp(b_last - b)).T
    decay_t = jnp.exp(b_last).T
    bd = bd_ref[...]
    ii = lax.broadcasted_iota(jnp.int32, (GLA_CHUNK, GLA_CHUNK), 0)
    jj = lax.broadcasted_iota(jnp.int32, (GLA_CHUNK, GLA_CHUNK), 1)
    causal = (jj >= ii) if rev else (jj <= ii)
    lane_k = lax.broadcasted_iota(jnp.int32, (GLA_CHUNK, W_GLA_K), 1) // GLA_DK
    lane_v = lax.broadcasted_iota(jnp.int32, (GLA_CHUNK, 256), 1) // GLA_DV
    nchunk = TILE // GLA_CHUNK
    for ci in range(nchunk):
        c = nchunk - 1 - ci if rev else ci
        rs = slice(c * GLA_CHUNK, (c + 1) * GLA_CHUNK)
        qcc, kic, vc = qc[rs], kinv[rs], v[rs]
        s_state = s_sc[...]
        o = _dot(qcc, s_state)
        for h in range(GLA_HEADS):
            att = _dot_nt(jnp.where(lane_k == h, qcc, 0.0), kic)
            att = jnp.where(causal, att, 0.0)
            o = o + jnp.where(lane_v == h, _dot(att, vc), 0.0)
        o_ref[rs, :] = o
        kv = _dot(kdec_t[:, rs], vc) * bd
        s_sc[...] = decay_t[:, c * GLA_CHUNK:c * GLA_CHUNK + 1] * s_state + kv


def _gla_scan(gqk, gv, ga, wa, ba, tri, sel, bd, rev):
    t_all = gqk.shape[0]
    nt = t_all // TILE
    tix = lambda j: (_scan_tile_index(j, nt, rev), 0)
    return pl.pallas_call(
        functools.partial(_gla_kernel, rev=rev),
        grid=(nt,),
        in_specs=[pl.BlockSpec((TILE, 256), tix), pl.BlockSpec((TILE, 256), tix), pl.BlockSpec((TILE, 128), tix),
                  _const_spec((128, W_GLA_K)), _const_spec((1, W_GLA_K)),
                  _const_spec((TILE, TILE)), _const_spec((TILE, TILE)), _const_spec((W_GLA_K, 256))],
        out_specs=pl.BlockSpec((TILE, 256), tix),
        out_shape=jax.ShapeDtypeStruct((t_all, 256), F32),
        scratch_shapes=[pltpu.VMEM((W_GLA_K, 256), F32)],
        compiler_params=_params("arbitrary"),
        name="gla_rev" if rev else "gla_fwd",
    )(gqk, gv, ga, wa, ba, tri, sel, bd)


def _merge_kernel(x_ref, mod_ref, u_ref, yf_ref, yb_ref, d_ref, wglu_ref, hf_ref, hb_ref, ly_ref,
                  att_ref, dan_ref, gf_ref, gb_ref, gg_ref, gn_ref, grp_ref, gates_ref, wb_ref, wo_ref,
                  o_ref, *, da_scale):
    z = jax.nn.gelu(u_ref[...] * d_ref[...] + yf_ref[...] + yb_ref[...])
    ya = z * jax.nn.sigmoid(_dot(z.astype(BF16), wglu_ref[...]))
    yb = (hf_ref[...] + hb_ref[...]) * jax.nn.gelu(ly_ref[...])
    og = gf_ref[...] + gb_ref[...]
    ms = _dot(og * og, grp_ref[...], HIGHEST)
    gg = gg_ref[...]
    yd = og * lax.rsqrt(ms + EPS) * gn_ref[...] * (gg * jax.nn.sigmoid(gg))

    def gate(n):
        return jax.nn.sigmoid(gates_ref[:, n * D_MODEL:(n + 1) * D_MODEL])

    acc = gate(0) * _dot(ya.astype(BF16), wb_ref[0])
    acc = acc + gate(1) * _dot(yb.astype(BF16), wb_ref[1])
    pc = jnp.zeros((TILE, D_MODEL), F32)
    for h in range(DA_HEADS):
        yc = _rms(att_ref[h], dan_ref[...]) * da_scale
        pc = pc + _dot(yc.astype(BF16), wb_ref[2, h * DA_VDIM:(h + 1) * DA_VDIM, :])
    acc = acc + gate(2) * pc
    acc = acc + gate(3) * _dot(yd.astype(BF16), wb_ref[3])
    o_ref[...] = x_ref[...] + mod_ref[0, 2:3] * _dot(acc.astype(BF16), wo_ref[...])


def _merge(xa, mods, u, yf, yb, ssm_d, w_glu, hf, hb, ly, att, da_norm, gf, gb, gg, gla_norm, grp64, gates,
           w_branch, w_out, da_scale):
    t_all = xa.shape[0]
    nt = t_all // TILE
    row = lambda w: pl.BlockSpec((TILE, w), lambda i: (i, 0))
    vec = lambda w: _const_spec((1, w))
    return pl.pallas_call(
        functools.partial(_merge_kernel, da_scale=da_scale),
        grid=(nt,),
        in_specs=[row(D_MODEL), pl.BlockSpec((1, SUBLANES, D_MODEL), lambda i: (jnp.minimum(i, 1), 0, 0)),
                  row(256), row(256), row(256), vec(256), _const_spec((256, 256)),
                  row(256), row(256), row(256),
                  pl.BlockSpec((DA_HEADS, TILE, DA_VDIM), lambda i: (0, i, 0)), vec(DA_VDIM),
                  row(256), row(256), row(256), vec(256), _const_spec((256, 256)), row(W_GATES),
                  _const_spec((N_BRANCH, W_BRANCH, D_MODEL)), _const_spec((D_MODEL, D_MODEL))],
        out_specs=row(D_MODEL),
        out_shape=jax.ShapeDtypeStruct((t_all, D_MODEL), F32),
        compiler_params=_params("arbitrary"),
        name="merge",
    )(xa, mods, u, yf, yb, ssm_d, w_glu, hf, hb, ly, att, da_norm, gf, gb, gg, gla_norm, grp64, gates,
      w_branch, w_out)


def _ffn_kernel(x_ref, xp_ref, xn_ref, mod_ref, g_ref, wa_ref, wg_ref, cw_ref, cb_ref, wd_ref, o_ref, ext, gt_sc,
                *, nt):
    i = pl.program_id(0)
    mod = mod_ref[0]
    g = g_ref[...]
    norm = lambda t: _rms(t, g) * (1.0 + mod[4:5]) + mod[3:4]
    prev_ok = (i >= 2).astype(F32)
    next_ok = jnp.logical_and(i != 0, i != nt - 1).astype(F32)
    x = x_ref[...]
    hn = norm(x)
    ext[0:HALO] = norm(xp_ref[...]) * prev_ok
    ext[HALO:HALO + TILE] = hn
    ext[HALO + TILE:] = norm(xn_ref[...]) * next_ok
    gt_sc[...] = _dot(ext[...].astype(BF16), wg_ref[...])
    gt = cb_ref[...] + sum(cw_ref[k:k + 1] * gt_sc[pl.ds(HALO - 1 + k, TILE), :] for k in range(3))
    a = _dot(hn.astype(BF16), wa_ref[...])
    y = _dot((jax.nn.gelu(gt) * a).astype(BF16), wd_ref[...])
    o_ref[...] = x + mod[5:6] * y


def _ffn(xa, mods, norm_g, w_up_a, w_up_g, conv_w, conv_b, w_down):
    t_all = xa.shape[0]
    nt = t_all // TILE
    per = TILE // HALO
    return pl.pallas_call(
        functools.partial(_ffn_kernel, nt=nt),
        grid=(nt,),
        in_specs=[pl.BlockSpec((TILE, D_MODEL), lambda i: (i, 0)),
                  pl.BlockSpec((HALO, D_MODEL), lambda i: (jnp.maximum(i * per - 1, 0), 0)),
                  pl.BlockSpec((HALO, D_MODEL), lambda i: (jnp.minimum((i + 1) * per, nt * per - 1), 0)),
                  pl.BlockSpec((1, SUBLANES, D_MODEL), lambda i: (jnp.minimum(i, 1), 0, 0)),
                  _const_spec((1, D_MODEL)), _const_spec((D_MODEL, D_FF)), _const_spec((D_MODEL, D_FF)),
                  _const_spec((3, D_FF)), _const_spec((1, D_FF)), _const_spec((D_FF, D_MODEL))],
        out_specs=pl.BlockSpec((TILE, D_MODEL), lambda i: (i, 0)),
        out_shape=jax.ShapeDtypeStruct((t_all, D_MODEL), F32),
        scratch_shapes=[pltpu.VMEM((TILE + 2 * HALO, D_MODEL), F32), pltpu.VMEM((TILE + 2 * HALO, D_FF), F32)],
        compiler_params=_params("arbitrary"),
        name="convffn",
    )(xa, xa, xa, mods, norm_g, w_up_a, w_up_g, conv_w, conv_b, w_down)


def _rope_tables(n_lat, n_ctx):
    pos = jnp.arange(n_lat)
    row, col = (pos // GRID_W).astype(F32), (pos % GRID_W).astype(F32)
    half = DA_HEAD // 2
    inv = 1.0 / (ROPE_BASE ** (jnp.arange(0, half, 2, dtype=F32) / half))
    ang = jnp.stack([row[:, None] * inv, col[:, None] * inv], axis=1)
    cos, sin = jnp.cos(ang), jnp.sin(ang)
    cos32 = jnp.stack([cos, cos], axis=2).reshape(n_lat, DA_HEAD)
    sin32 = jnp.stack([-sin, sin], axis=2).reshape(n_lat, DA_HEAD)
    cos_t = jnp.concatenate([jnp.ones((n_ctx, DA_HEAD), F32), cos32], axis=0)
    sin_t = jnp.concatenate([jnp.zeros((n_ctx, DA_HEAD), F32), sin32], axis=0)
    return jnp.tile(cos_t, (1, N_PAIR)), jnp.tile(sin_t, (1, N_PAIR))


def _group_mean_matrix(width, group):
    idx = jnp.arange(width) // group
    return (idx[:, None] == idx[None, :]).astype(F32) / group


def _gla_chunk_matrices(rev):
    t = jnp.arange(TILE)
    same = (t[:, None] // GLA_CHUNK) == (t[None, :] // GLA_CHUNK)
    order = (t[None, :] >= t[:, None]) if rev else (t[None, :] <= t[:, None])
    return (same & order).astype(F32), same.astype(F32)


def kernel(x, c, ctx, c_ctx, w_ada, b_ada, norm1_g, norm2_g, w_in, ssm_lam_re, ssm_lam_im, ssm_log_step, ssm_b_re, ssm_b_im, ssm_c_re, ssm_c_im, ssm_d, ssm_w_glu, lru_conv_w, lru_conv_b, lru_wr, lru_br, lru_wi, lru_bi, lru_lam, da_q_norm, da_k_norm, da_lam, da_out_norm, gla_wa2, gla_ba, gla_out_norm, w_branch, w_out, w_up, ffn_conv_w, ffn_conv_b, w_down):
    depth = w_in.shape[0]
    n_lat, n_ctx = x.shape[1], ctx.shape[1]
    assert x.shape[0] == 1 and n_ctx == TILE and n_lat % TILE == 0
    xa = jnp.concatenate([ctx[0], x[0]], axis=0).astype(F32)

    cond = jnp.zeros((SUBLANES, D_MODEL), F32).at[0].set(c_ctx).at[1].set(c[0])
    ada = _adaln(cond, w_ada, b_ada)
    mods_all = jnp.pad(ada[:, :2].reshape(depth, 2, 6, D_MODEL), ((0, 0), (0, 0), (0, SUBLANES - 6), (0, 0)))

    cos_t, sin_t = _rope_tables(n_lat, n_ctx)
    grp32 = _group_mean_matrix(256, DA_HEAD)
    grp64 = _group_mean_matrix(256, GLA_DV)
    bd_mask = ((jnp.arange(W_GLA_K)[:, None] // GLA_DK) == (jnp.arange(256)[None, :] // GLA_DV)).astype(F32)
    gla_mats = [_gla_chunk_matrices(rev) for rev in (False, True)]
    tk = TILE * max(d for d in range(1, 9) if ((n_ctx + n_lat) // TILE) % d == 0)
    tq = min(512, n_lat)

    for l in range(depth):
        mods = mods_all[l]
        w_l = w_in[l]
        w_main = w_l[:, :W_MAIN].astype(BF16)
        w_a = jnp.pad(w_l[:, W_MAIN:W_MAIN + 2 * GLA_RANK], ((0, 0), (0, 128 - 2 * GLA_RANK))).astype(BF16)
        w_gates = w_l[:, W_MAIN + 2 * GLA_RANK:].astype(BF16)
        qg = jnp.tile(da_q_norm[l], N_PAIR)[None]
        kg = jnp.tile(da_k_norm[l], N_PAIR)[None]
        u, lx, ly, q, kt, v, gqk, gv, gg, ga, gates = _inproj(
            xa, mods, norm1_g[l][None], w_main, w_a, w_gates, cos_t, sin_t, qg, kg, grp32)

        ys, hs, gs = [], [], []
        for di, rev in enumerate((False, True)):
            bcat, ccat, tabs = _s5_tables(ssm_lam_re[l, di], ssm_lam_im[l, di], ssm_log_step[l, di],
                                          ssm_b_re[l, di], ssm_b_im[l, di], ssm_c_re[l, di], ssm_c_im[l, di], rev)
            ys.append(_s5_scan(u, bcat, ccat, tabs, rev))
            hs.append(_lru_scan(lx, lru_conv_w[l], lru_conv_b[l][None], _block_diag(lru_wr[l, di]),
                                lru_br[l, di][None], _block_diag(lru_wi[l, di]), lru_bi[l, di][None],
                                lru_lam[l, di][None], rev))
            wa = jnp.zeros((128, W_GLA_K), F32).at[di * GLA_RANK:(di + 1) * GLA_RANK].set(gla_wa2[l, di])
            tri, sel = gla_mats[di]
            gs.append(_gla_scan(gqk, gv, ga, wa, gla_ba[l, di][None], tri, sel, bd_mask, rev))

        lp = da_lam[l].astype(F32)
        lam_init = 0.8 - 0.6 * math.exp(-0.3 * l)
        lam = (jnp.exp(jnp.sum(lp[0] * lp[1])) - jnp.exp(jnp.sum(lp[2] * lp[3])) + lam_init).reshape(1, 1)
        att_lat = _attention(lam, q[n_ctx:], kt, v, tq, tk)
        att_ctx = _attention(lam, q[:n_ctx], kt[:, :n_ctx], v[:, :n_ctx], n_ctx, n_ctx)
        att = jnp.concatenate([att_ctx, att_lat], axis=1)

        xa = _merge(xa, mods, u, ys[0], ys[1], ssm_d[l][None], ssm_w_glu[l].astype(BF16), hs[0], hs[1], ly,
                    att, da_out_norm[l][None], gs[0], gs[1], gg, jnp.tile(gla_out_norm[l], GLA_HEADS)[None],
                    grp64, gates, w_branch[l].astype(BF16), w_out[l].astype(BF16), 1.0 - lam_init)
        xa = _ffn(xa, mods, norm2_g[l][None], w_up[l][:, :D_FF].astype(BF16), w_up[l][:, D_FF:].astype(BF16),
                  ffn_conv_w[l], ffn_conv_b[l][None], w_down[l].astype(BF16))
    return xa[n_ctx:][None].astype(x.dtype)
```

```python
import functools
import math

import jax
import jax.numpy as jnp
from jax import lax
from jax.experimental import pallas as pl
from jax.experimental.pallas import tpu as pltpu

F32 = jnp.float32
BF16 = jnp.bfloat16
HIGHEST = lax.Precision.HIGHEST

D_MODEL = 1024
EPS = 1e-6
GRID_W = 64
TILE = 256
SUBLANES = 8
HALO = SUBLANES
VMEM_LIMIT = 56 * 1024 * 1024

SSM_GROUPS, SSM_GROUP, SSM_STATE = 16, 16, 64
W_SSM = 256
N_STATE = SSM_GROUPS * SSM_STATE
W_LRU, LRU_BLOCKS, LRU_BLOCK, LRU_C = 256, 8, 32, 8.0
DA_HEADS, DA_HEAD, DA_VDIM = 4, 32, 64
N_PAIR = 2 * DA_HEADS
ROPE_BASE = 10000.0
GLA_HEADS, GLA_DK, GLA_DV, GLA_RANK, GLA_TAU, GLA_CHUNK = 4, 32, 64, 16, 16.0, 64
W_GLA_K = GLA_HEADS * GLA_DK
N_BRANCH, W_BRANCH, D_FF = 4, 256, 2816
W_MAIN = 2304
W_GATES = N_BRANCH * D_MODEL
Q_SCALE = DA_HEAD ** -0.5 * math.log2(math.e)
MAX_SHIFT_SPREAD = 100.0


def _dot(a, b, precision=None):
    return jnp.dot(a, b, preferred_element_type=F32, precision=precision)


def _dot_nt(a, b, precision=None):
    return lax.dot_general(a, b, (((1,), (1,)), ((), ())), preferred_element_type=F32, precision=precision)


def _params(*sem):
    return pltpu.CompilerParams(dimension_semantics=sem, vmem_limit_bytes=VMEM_LIMIT)


def _const_spec(shape):
    nd = len(shape)
    return pl.BlockSpec(shape, lambda *_: (0,) * nd)


def _rms(x, g):
    return x * lax.rsqrt(jnp.mean(x * x, axis=-1, keepdims=True) + EPS) * g


def _adaln_kernel(c_ref, w_ref, b_ref, o_ref):
    c = c_ref[...]
    o_ref[0] = _dot(c * jax.nn.sigmoid(c), w_ref[0], HIGHEST) + b_ref[0]


def _adaln(cond, w_ada, b_ada):
    depth, _, n_out = w_ada.shape
    blk = 1536
    return pl.pallas_call(
        _adaln_kernel,
        grid=(depth, n_out // blk),
        in_specs=[pl.BlockSpec((SUBLANES, D_MODEL), lambda l, n: (0, 0)),
                  pl.BlockSpec((1, D_MODEL, blk), lambda l, n: (l, 0, n)),
                  pl.BlockSpec((1, 1, blk), lambda l, n: (l, 0, n))],
        out_specs=pl.BlockSpec((1, SUBLANES, blk), lambda l, n: (l, 0, n)),
        out_shape=jax.ShapeDtypeStruct((depth, SUBLANES, n_out), F32),
        compiler_params=_params("arbitrary", "arbitrary"),
        name="adaln",
    )(cond, w_ada, b_ada.reshape(depth, 1, n_out))


def _inproj_kernel(x_ref, mod_ref, g_ref, wm_ref, wa_ref, wg_ref, cos_ref, sin_ref, qg_ref, kg_ref, grp_ref,
                   u_ref, lx_ref, ly_ref, q_ref, kt_ref, v_ref, gqk_ref, gv_ref, gg_ref, ga_ref, gates_ref,
                   qmax_ref, kmax_ref):
    x = x_ref[...]
    mod = mod_ref[0]
    hn = _rms(x, g_ref[...]) * (1.0 + mod[1:2]) + mod[0:1]
    hb = hn.astype(BF16)
    z = _dot(hb, wm_ref[...])
    u_ref[...] = z[:, 0:256]
    lx_ref[...] = z[:, 256:512]
    ly_ref[...] = z[:, 512:768]
    gqk_ref[...] = z[:, 1536:1792]
    gv_ref[...] = z[:, 1792:2048]
    gg_ref[...] = z[:, 2048:2304]
    ga_ref[...] = _dot(hb, wa_ref[...])
    gates_ref[...] = _dot(hb, wg_ref[...])

    cos, sin = cos_ref[...], sin_ref[...]
    lane = lax.broadcasted_iota(jnp.int32, (TILE, 256), 1)
    first_half = (lane % 16) < 8
    grp = grp_ref[...]

    def qk_norm_rope(t, g):
        ms = _dot(t * t, grp, HIGHEST)
        tn = t * lax.rsqrt(ms + EPS) * g
        partner = jnp.where(first_half, pltpu.roll(tn, 256 - 8, 1), pltpu.roll(tn, 8, 1))
        return tn * cos + partner * sin

    def max_sq_norm(tb):
        t = tb.astype(F32)
        return jnp.max(_dot(t * t, grp, HIGHEST) * DA_HEAD, axis=0, keepdims=True)

    @pl.when(pl.program_id(0) == 0)
    def _():
        qmax_ref[...] = jnp.zeros_like(qmax_ref)
        kmax_ref[...] = jnp.zeros_like(kmax_ref)

    qb = (qk_norm_rope(z[:, 768:1024], qg_ref[...]) * Q_SCALE).astype(BF16)
    kb = qk_norm_rope(z[:, 1024:1280], kg_ref[...]).astype(BF16)
    qmax_ref[...] = jnp.maximum(qmax_ref[...], max_sq_norm(qb))
    kmax_ref[...] = jnp.maximum(kmax_ref[...], max_sq_norm(kb))
    for p in range(N_PAIR):
        q_ref[p] = qb[:, p * DA_HEAD:(p + 1) * DA_HEAD]
    kt_ref[...] = kb.T
    lane_v = lax.broadcasted_iota(jnp.int32, (TILE, 128), 1)
    tail = (lane_v == DA_VDIM).astype(F32)
    for h in range(DA_HEADS):
        vh = z[:, 1280 + 128 * (h // 2):1280 + 128 * (h // 2 + 1)]
        if h % 2:
            vh = pltpu.roll(vh, DA_VDIM, 1)
        v_ref[h] = jnp.where(lane_v < DA_VDIM, vh, tail).astype(BF16)


def _inproj(xa, mods, norm_g, w_main, w_a, w_gates, cos_t, sin_t, qg, kg, grp32):
    t_all = xa.shape[0]
    nt = t_all // TILE
    row = lambda w: pl.BlockSpec((TILE, w), lambda i: (i, 0))
    f = lambda w: jax.ShapeDtypeStruct((t_all, w), F32)
    return pl.pallas_call(
        _inproj_kernel,
        grid=(nt,),
        in_specs=[row(D_MODEL),
                  pl.BlockSpec((1, SUBLANES, D_MODEL), lambda i: (jnp.minimum(i, 1), 0, 0)),
                  _const_spec((1, D_MODEL)),
                  _const_spec((D_MODEL, W_MAIN)), _const_spec((D_MODEL, 128)), _const_spec((D_MODEL, W_GATES)),
                  row(256), row(256), _const_spec((1, 256)), _const_spec((1, 256)), _const_spec((256, 256))],
        out_specs=[row(256), row(256), row(256),
                   pl.BlockSpec((N_PAIR, TILE, DA_HEAD), lambda i: (0, i, 0)),
                   pl.BlockSpec((256, TILE), lambda i: (0, i)),
                   pl.BlockSpec((DA_HEADS, TILE, 128), lambda i: (0, i, 0)),
                   row(256), row(256), row(256), row(128), row(W_GATES),
                   _const_spec((SUBLANES, 256)), _const_spec((SUBLANES, 256))],
        out_shape=[f(256), f(256), f(256), jax.ShapeDtypeStruct((N_PAIR, t_all, DA_HEAD), BF16),
                   jax.ShapeDtypeStruct((256, t_all), BF16),
                   jax.ShapeDtypeStruct((DA_HEADS, t_all, 128), BF16),
                   f(256), f(256), f(256), f(128), f(W_GATES),
                   jax.ShapeDtypeStruct((SUBLANES, 256), F32), jax.ShapeDtypeStruct((SUBLANES, 256), F32)],
        compiler_params=_params("arbitrary"),
        name="inproj",
    )(xa, mods, norm_g, w_main, w_a, w_gates, cos_t, sin_t, qg, kg, grp32)


def _scan_tile_index(j, nt, rev):
    return jnp.where(j == 0, 0, nt - j) if rev else j


def _s5_kernel(u_ref, b_ref, c_ref, tab_ref, y_ref, hre, him, car, *, rev):
    j = pl.program_id(0)

    @pl.when(j == 0)
    def _():
        car[...] = jnp.zeros_like(car)

    bu = _dot(u_ref[...].astype(BF16), b_ref[...])
    hre[...] = bu[:, :N_STATE]
    him[...] = bu[:, N_STATE:]
    nblk = TILE // SUBLANES

    def body(b, carry):
        cr, ci = carry
        blk = nblk - 1 - b if rev else b
        r0 = pl.multiple_of(blk * SUBLANES, SUBLANES)
        xr = hre[pl.ds(r0, SUBLANES), :]
        xi = him[pl.ds(r0, SUBLANES), :]
        for k, s in enumerate((1, 2, 4)):
            shift = SUBLANES - s if rev else s
            sr = pltpu.roll(xr, shift, 0)
            si = pltpu.roll(xi, shift, 0)
            mr, mi = tab_ref[2 * k], tab_ref[2 * k + 1]
            xr, xi = xr + mr * sr - mi * si, xi + mr * si + mi * sr
        pr, pi_ = tab_ref[6], tab_ref[7]
        xr, xi = xr + pr * cr - pi_ * ci, xi + pr * ci + pi_ * cr
        hre[pl.ds(r0, SUBLANES), :] = xr
        him[pl.ds(r0, SUBLANES), :] = xi
        last = 0 if rev else SUBLANES - 1
        return (jnp.broadcast_to(xr[last:last + 1], xr.shape), jnp.broadcast_to(xi[last:last + 1], xi.shape))

    cr, ci = lax.fori_loop(0, nblk, body, (car[0], car[1]))
    car[0] = cr
    car[1] = ci
    y_ref[...] = (_dot(hre[...].astype(BF16), c_ref[:N_STATE]) + _dot(him[...].astype(BF16), c_ref[N_STATE:]))


def _s5_scan(u, bcat, ccat, tabs, rev):
    t_all = u.shape[0]
    nt = t_all // TILE
    tix = lambda j: (_scan_tile_index(j, nt, rev), 0)
    return pl.pallas_call(
        functools.partial(_s5_kernel, rev=rev),
        grid=(nt,),
        in_specs=[pl.BlockSpec((TILE, W_SSM), tix), _const_spec((W_SSM, 2 * N_STATE)),
                  _const_spec((2 * N_STATE, W_SSM)), _const_spec((8, SUBLANES, N_STATE))],
        out_specs=pl.BlockSpec((TILE, W_SSM), tix),
        out_shape=jax.ShapeDtypeStruct((t_all, W_SSM), F32),
        scratch_shapes=[pltpu.VMEM((TILE, N_STATE), F32), pltpu.VMEM((TILE, N_STATE), F32),
                        pltpu.VMEM((2, SUBLANES, N_STATE), F32)],
        compiler_params=_params("arbitrary"),
        name="s5_rev" if rev else "s5_fwd",
    )(u, bcat, ccat, tabs)


def _s5_tables(lam_re, lam_im, log_step, b_re, b_im, c_re, c_im, rev):
    dt = jnp.exp(log_step.astype(F32))[:, None]
    lre, lim = lam_re.astype(F32) * dt, lam_im.astype(F32) * dt
    mag = jnp.exp(lre)
    lb_re, lb_im = mag * jnp.cos(lim), mag * jnp.sin(lim)
    num_re, num_im = lb_re - 1.0, lb_im
    den = lam_re * lam_re + lam_im * lam_im
    k_re = (num_re * lam_re + num_im * lam_im) / den
    k_im = (num_im * lam_re - num_re * lam_im) / den
    bb_re = k_re[..., None] * b_re - k_im[..., None] * b_im
    bb_im = k_re[..., None] * b_im + k_im[..., None] * b_re
    eye = jnp.eye(SSM_GROUPS, dtype=F32)
    blk_in = lambda t: jnp.einsum('gpc,gh->gchp', t, eye).reshape(W_SSM, N_STATE)
    blk_out = lambda t: jnp.einsum('gcp,gh->gphc', t, eye).reshape(N_STATE, W_SSM)
    bcat = jnp.concatenate([blk_in(bb_re), blk_in(bb_im)], axis=1).astype(BF16)
    ccat = jnp.concatenate([blk_out(c_re.astype(F32)), -blk_out(c_im.astype(F32))], axis=0).astype(BF16)

    def power(k):
        k = k[:, None].astype(F32)
        m = jnp.exp(k * lre.reshape(1, -1))
        return m * jnp.cos(k * lim.reshape(1, -1)), m * jnp.sin(k * lim.reshape(1, -1))

    r = jnp.arange(SUBLANES)
    tabs = []
    for s in (1, 2, 4):
        pr, pi_ = power(jnp.full((SUBLANES,), s))
        keep = ((r + s <= SUBLANES - 1) if rev else (r >= s))[:, None]
        tabs += [jnp.where(keep, pr, 0.0), jnp.where(keep, pi_, 0.0)]
    tabs += list(power((SUBLANES - r) if rev else (r + 1)))
    return bcat, ccat, jnp.stack(tabs)


def _lru_kernel(x_ref, xp_ref, xn_ref, cw_ref, cb_ref, wr_ref, br_ref, wi_ref, bi_ref, lam_ref,
                h_ref, ext, a_sc, b_sc, car, *, rev, nt):
    j = pl.program_id(0)
    ti = _scan_tile_index(j, nt, rev)

    @pl.when(j == 0)
    def _():
        car[...] = jnp.zeros_like(car)

    prev_ok = (ti >= 2).astype(F32)
    next_ok = jnp.logical_and(ti != 0, ti != nt - 1).astype(F32)
    ext[0:HALO] = xp_ref[...] * prev_ok
    ext[HALO:HALO + TILE] = x_ref[...]
    ext[HALO + TILE:] = xn_ref[...] * next_ok
    xc = cb_ref[...] + sum(cw_ref[k:k + 1] * ext[pl.ds(HALO - 2 + k, TILE), :] for k in range(4))
    r = jax.nn.sigmoid(_dot(xc, wr_ref[...], HIGHEST) + br_ref[...])
    i = jax.nn.sigmoid(_dot(xc, wi_ref[...], HIGHEST) + bi_ref[...])
    log_a = LRU_C * r * jax.nn.log_sigmoid(lam_ref[...])
    a = jnp.exp(log_a)
    a_sc[...] = a
    b_sc[...] = jnp.sqrt(-jnp.tanh(log_a) * (a * a + 1.0)) * (i * xc)
    nblk = TILE // SUBLANES
    rows = lax.broadcasted_iota(jnp.int32, (SUBLANES, W_LRU), 0)

    def body(b, carry):
        blk = nblk - 1 - b if rev else b
        r0 = pl.multiple_of(blk * SUBLANES, SUBLANES)
        a = a_sc[pl.ds(r0, SUBLANES), :]
        x = b_sc[pl.ds(r0, SUBLANES), :]
        for s in (1, 2, 4):
            shift = SUBLANES - s if rev else s
            keep = (rows + s <= SUBLANES - 1) if rev else (rows >= s)
            a_s = jnp.where(keep, pltpu.roll(a, shift, 0), 1.0)
            x_s = jnp.where(keep, pltpu.roll(x, shift, 0), 0.0)
            a, x = a * a_s, a * x_s + x
        h = x + a * carry
        b_sc[pl.ds(r0, SUBLANES), :] = h
        last = 0 if rev else SUBLANES - 1
        return jnp.broadcast_to(h[last:last + 1], h.shape)

    car[...] = lax.fori_loop(0, nblk, body, car[...])
    h_ref[...] = b_sc[...]


def _lru_scan(lx, conv_w, conv_b, wr, br, wi, bi, lam, rev):
    t_all = lx.shape[0]
    nt = t_all // TILE
    per = TILE // HALO
    tix = lambda j: (_scan_tile_index(j, nt, rev), 0)
    pix = lambda j: (jnp.maximum(_scan_tile_index(j, nt, rev) * per - 1, 0), 0)
    nix = lambda j: (jnp.minimum((_scan_tile_index(j, nt, rev) + 1) * per, nt * per - 1), 0)
    vec = _const_spec((1, W_LRU))
    return pl.pallas_call(
        functools.partial(_lru_kernel, rev=rev, nt=nt),
        grid=(nt,),
        in_specs=[pl.BlockSpec((TILE, W_LRU), tix), pl.BlockSpec((HALO, W_LRU), pix), pl.BlockSpec((HALO, W_LRU), nix),
                  _const_spec((4, W_LRU)), vec, _const_spec((W_LRU, W_LRU)), vec, _const_spec((W_LRU, W_LRU)), vec, vec],
        out_specs=pl.BlockSpec((TILE, W_LRU), tix),
        out_shape=jax.ShapeDtypeStruct((t_all, W_LRU), F32),
        scratch_shapes=[pltpu.VMEM((TILE + 2 * HALO, W_LRU), F32), pltpu.VMEM((TILE, W_LRU), F32),
                        pltpu.VMEM((TILE, W_LRU), F32), pltpu.VMEM((SUBLANES, W_LRU), F32)],
        compiler_params=_params("arbitrary"),
        name="lru_rev" if rev else "lru_fwd",
    )(lx, lx, lx, conv_w, conv_b, wr, br, wi, bi, lam)


def _block_diag(w):
    n, a, b = w.shape
    return jnp.einsum('nab,nm->namb', w, jnp.eye(n, dtype=w.dtype)).reshape(n * a, n * b)


def _attn_kernel(sc_ref, q_ref, kt_ref, v_ref, o_ref, m_sc, acc_sc, *, nkv, online):
    j = pl.program_id(1)

    @pl.when(j == 0)
    def _():
        acc_sc[...] = jnp.zeros_like(acc_sc)
        for p in range(N_PAIR):
            if online:
                m_sc[p] = jnp.full(m_sc.shape[1:], -jnp.inf, F32)
            else:
                qf = q_ref[p].astype(F32)
                m_sc[p] = jnp.sqrt(jnp.sum(qf * qf, axis=-1, keepdims=True)) * sc_ref[0, 1 + p]

    for p in range(N_PAIR):
        s = _dot(q_ref[p], kt_ref[p * DA_HEAD:(p + 1) * DA_HEAD, :])
        if online:
            m_old = m_sc[p]
            m_new = jnp.maximum(m_old, jnp.max(s, axis=-1, keepdims=True))
            m_sc[p] = m_new
            e = jnp.exp2(s - m_new).astype(BF16)
            acc_sc[p] = jnp.exp2(m_old - m_new) * acc_sc[p] + _dot(e, v_ref[p // 2])
        else:
            e = jnp.exp2(s - m_sc[p]).astype(BF16)
            acc_sc[p] += _dot(e, v_ref[p // 2])

    @pl.when(j == nkv - 1)
    def _():
        lam = sc_ref[0, 0]
        for h in range(DA_HEADS):
            a0, a1 = acc_sc[2 * h], acc_sc[2 * h + 1]
            o_ref[h] = (a0[:, :DA_VDIM] / a0[:, DA_VDIM:DA_VDIM + 1]
                        - lam * (a1[:, :DA_VDIM] / a1[:, DA_VDIM:DA_VDIM + 1]))


def _attention(scal, q, kt, v, tq, tk, online):
    n_q = q.shape[1]
    nq, nkv = n_q // tq, kt.shape[1] // tk
    return pl.pallas_call(
        functools.partial(_attn_kernel, nkv=nkv, online=online),
        grid=(nq, nkv),
        in_specs=[pl.BlockSpec(memory_space=pltpu.SMEM),
                  pl.BlockSpec((N_PAIR, tq, DA_HEAD), lambda i, j: (0, i, 0)),
                  pl.BlockSpec((256, tk), lambda i, j: (0, j)),
                  pl.BlockSpec((DA_HEADS, tk, 128), lambda i, j: (0, j, 0))],
        out_specs=pl.BlockSpec((DA_HEADS, tq, DA_VDIM), lambda i, j: (0, i, 0)),
        out_shape=jax.ShapeDtypeStruct((DA_HEADS, n_q, DA_VDIM), F32),
        scratch_shapes=[pltpu.VMEM((N_PAIR, tq, 1), F32), pltpu.VMEM((N_PAIR, tq, 128), F32)],
        compiler_params=_params("arbitrary", "arbitrary"),
        name="diff_attn_online" if online else "diff_attn",
    )(scal, q, kt, v)


def _gla_kernel(qk_ref, v_ref, ga_ref, wa_ref, ba_ref, tri_ref, sel_ref, bd_ref, o_ref, s_sc, *, rev):
    j = pl.program_id(0)

    @pl.when(j == 0)
    def _():
        s_sc[...] = jnp.zeros_like(s_sc)

    q = qk_ref[:, :W_GLA_K] * (GLA_DK ** -0.5)
    k = qk_ref[:, W_GLA_K:]
    v = v_ref[...]
    lg = jax.nn.log_sigmoid(_dot(ga_ref[...], wa_ref[...], HIGHEST) + ba_ref[...]) / GLA_TAU
    b = _dot(tri_ref[...], lg, HIGHEST)
    b_last = _dot(sel_ref[...], lg, HIGHEST)
    qc = q * jnp.exp(b)
    kinv = k * jnp.exp(-b)
    kdec_t = (k * jnp.exp(b_last - b)).T
    decay_t = jnp.exp(b_last).T
    bd = bd_ref[...]
    ii = lax.broadcasted_iota(jnp.int32, (GLA_CHUNK, GLA_CHUNK), 0)
    jj = lax.broadcasted_iota(jnp.int32, (GLA_CHUNK, GLA_CHUNK), 1)
    causal = (jj >= ii) if rev else (jj <= ii)
    lane_k = lax.broadcasted_iota(jnp.int32, (GLA_CHUNK, W_GLA_K), 1) // GLA_DK
    lane_v = lax.broadcasted_iota(jnp.int32, (GLA_CHUNK, 256), 1) // GLA_DV
    nchunk = TILE // GLA_CHUNK
    for ci in range(nchunk):
        c = nchunk - 1 - ci if rev else ci
        rs = slice(c * GLA_CHUNK, (c + 1) * GLA_CHUNK)
        qcc, kic, vc = qc[rs], kinv[rs], v[rs]
        s_state = s_sc[...]
        o = _dot(qcc, s_state)
        for h in range(GLA_HEADS):
            att = _dot_nt(jnp.where(lane_k == h, qcc, 0.0), kic)
            att = jnp.where(causal, att, 0.0)
            o = o + jnp.where(lane_v == h, _dot(att, vc), 0.0)
        o_ref[rs, :] = o
        kv = _dot(kdec_t[:, rs], vc) * bd
        s_sc[...] = decay_t[:, c * GLA_CHUNK:c * GLA_CHUNK + 1] * s_state + kv


def _gla_scan(gqk, gv, ga, wa, ba, tri, sel, bd, rev):
    t_all = gqk.shape[0]
    nt = t_all // TILE
    tix = lambda j: (_scan_tile_index(j, nt, rev), 0)
    return pl.pallas_call(
        functools.partial(_gla_kernel, rev=rev),
        grid=(nt,),
        in_specs=[pl.BlockSpec((TILE, 256), tix), pl.BlockSpec((TILE, 256), tix), pl.BlockSpec((TILE, 128), tix),
                  _const_spec((128, W_GLA_K)), _const_spec((1, W_GLA_K)),
                  _const_spec((TILE, TILE)), _const_spec((TILE, TILE)), _const_spec((W_GLA_K, 256))],
        out_specs=pl.BlockSpec((TILE, 256), tix),
        out_shape=jax.ShapeDtypeStruct((t_all, 256), F32),
        scratch_shapes=[pltpu.VMEM((W_GLA_K, 256), F32)],
        compiler_params=_params("arbitrary"),
        name="gla_rev" if rev else "gla_fwd",
    )(gqk, gv, ga, wa, ba, tri, sel, bd)


def _merge_kernel(x_ref, mod_ref, u_ref, yf_ref, yb_ref, d_ref, wglu_ref, hf_ref, hb_ref, ly_ref,
                  att_ref, dan_ref, gf_ref, gb_ref, gg_ref, gn_ref, grp_ref, gates_ref, wb_ref, wo_ref,
                  o_ref, *, da_scale):
    z = jax.nn.gelu(u_ref[...] * d_ref[...] + yf_ref[...] + yb_ref[...])
    ya = z * jax.nn.sigmoid(_dot(z.astype(BF16), wglu_ref[...]))
    yb = (hf_ref[...] + hb_ref[...]) * jax.nn.gelu(ly_ref[...])
    og = gf_ref[...] + gb_ref[...]
    ms = _dot(og * og, grp_ref[...], HIGHEST)
    gg = gg_ref[...]
    yd = og * lax.rsqrt(ms + EPS) * gn_ref[...] * (gg * jax.nn.sigmoid(gg))

    def gate(n):
        return jax.nn.sigmoid(gates_ref[:, n * D_MODEL:(n + 1) * D_MODEL])

    acc = gate(0) * _dot(ya.astype(BF16), wb_ref[0])
    acc = acc + gate(1) * _dot(yb.astype(BF16), wb_ref[1])
    pc = jnp.zeros((TILE, D_MODEL), F32)
    for h in range(DA_HEADS):
        yc = _rms(att_ref[h], dan_ref[...]) * da_scale
        pc = pc + _dot(yc.astype(BF16), wb_ref[2, h * DA_VDIM:(h + 1) * DA_VDIM, :])
    acc = acc + gate(2) * pc
    acc = acc + gate(3) * _dot(yd.astype(BF16), wb_ref[3])
    o_ref[...] = x_ref[...] + mod_ref[0, 2:3] * _dot(acc.astype(BF16), wo_ref[...])


def _merge(xa, mods, u, yf, yb, ssm_d, w_glu, hf, hb, ly, att, da_norm, gf, gb, gg, gla_norm, grp64, gates,
           w_branch, w_out, da_scale):
    t_all = xa.shape[0]
    nt = t_all // TILE
    row = lambda w: pl.BlockSpec((TILE, w), lambda i: (i, 0))
    vec = lambda w: _const_spec((1, w))
    return pl.pallas_call(
        functools.partial(_merge_kernel, da_scale=da_scale),
        grid=(nt,),
        in_specs=[row(D_MODEL), pl.BlockSpec((1, SUBLANES, D_MODEL), lambda i: (jnp.minimum(i, 1), 0, 0)),
                  row(256), row(256), row(256), vec(256), _const_spec((256, 256)),
                  row(256), row(256), row(256),
                  pl.BlockSpec((DA_HEADS, TILE, DA_VDIM), lambda i: (0, i, 0)), vec(DA_VDIM),
                  row(256), row(256), row(256), vec(256), _const_spec((256, 256)), row(W_GATES),
                  _const_spec((N_BRANCH, W_BRANCH, D_MODEL)), _const_spec((D_MODEL, D_MODEL))],
        out_specs=row(D_MODEL),
        out_shape=jax.ShapeDtypeStruct((t_all, D_MODEL), F32),
        compiler_params=_params("arbitrary"),
        name="merge",
    )(xa, mods, u, yf, yb, ssm_d, w_glu, hf, hb, ly, att, da_norm, gf, gb, gg, gla_norm, grp64, gates,
      w_branch, w_out)


def _ffn_kernel(x_ref, xp_ref, xn_ref, mod_ref, g_ref, wa_ref, wg_ref, cw_ref, cb_ref, wd_ref, o_ref, ext, gt_sc,
                *, nt):
    i = pl.program_id(0)
    mod = mod_ref[0]
    g = g_ref[...]
    norm = lambda t: _rms(t, g) * (1.0 + mod[4:5]) + mod[3:4]
    prev_ok = (i >= 2).astype(F32)
    next_ok = jnp.logical_and(i != 0, i != nt - 1).astype(F32)
    x = x_ref[...]
    hn = norm(x)
    ext[0:HALO] = norm(xp_ref[...]) * prev_ok
    ext[HALO:HALO + TILE] = hn
    ext[HALO + TILE:] = norm(xn_ref[...]) * next_ok
    gt_sc[...] = _dot(ext[...].astype(BF16), wg_ref[...])
    gt = cb_ref[...] + sum(cw_ref[k:k + 1] * gt_sc[pl.ds(HALO - 1 + k, TILE), :] for k in range(3))
    a = _dot(hn.astype(BF16), wa_ref[...])
    y = _dot((jax.nn.gelu(gt) * a).astype(BF16), wd_ref[...])
    o_ref[...] = x + mod[5:6] * y


def _ffn(xa, mods, norm_g, w_up_a, w_up_g, conv_w, conv_b, w_down):
    t_all = xa.shape[0]
    nt = t_all // TILE
    per = TILE // HALO
    return pl.pallas_call(
        functools.partial(_ffn_kernel, nt=nt),
        grid=(nt,),
        in_specs=[pl.BlockSpec((TILE, D_MODEL), lambda i: (i, 0)),
                  pl.BlockSpec((HALO, D_MODEL), lambda i: (jnp.maximum(i * per - 1, 0), 0)),
                  pl.BlockSpec((HALO, D_MODEL), lambda i: (jnp.minimum((i + 1) * per, nt * per - 1), 0)),
                  pl.BlockSpec((1, SUBLANES, D_MODEL), lambda i: (jnp.minimum(i, 1), 0, 0)),
                  _const_spec((1, D_MODEL)), _const_spec((D_MODEL, D_FF)), _const_spec((D_MODEL, D_FF)),
                  _const_spec((3, D_FF)), _const_spec((1, D_FF)), _const_spec((D_FF, D_MODEL))],
        out_specs=pl.BlockSpec((TILE, D_MODEL), lambda i: (i, 0)),
        out_shape=jax.ShapeDtypeStruct((t_all, D_MODEL), F32),
        scratch_shapes=[pltpu.VMEM((TILE + 2 * HALO, D_MODEL), F32), pltpu.VMEM((TILE + 2 * HALO, D_FF), F32)],
        compiler_params=_params("arbitrary"),
        name="convffn",
    )(xa, xa, xa, mods, norm_g, w_up_a, w_up_g, conv_w, conv_b, w_down)


def _rope_tables(n_lat, n_ctx):
    pos = jnp.arange(n_lat)
    row, col = (pos // GRID_W).astype(F32), (pos % GRID_W).astype(F32)
    half = DA_HEAD // 2
    inv = 1.0 / (ROPE_BASE ** (jnp.arange(0, half, 2, dtype=F32) / half))
    ang = jnp.stack([row[:, None] * inv, col[:, None] * inv], axis=1)
    cos, sin = jnp.cos(ang), jnp.sin(ang)
    cos32 = jnp.stack([cos, cos], axis=2).reshape(n_lat, DA_HEAD)
    sin32 = jnp.stack([-sin, sin], axis=2).reshape(n_lat, DA_HEAD)
    cos_t = jnp.concatenate([jnp.ones((n_ctx, DA_HEAD), F32), cos32], axis=0)
    sin_t = jnp.concatenate([jnp.zeros((n_ctx, DA_HEAD), F32), sin32], axis=0)
    return jnp.tile(cos_t, (1, N_PAIR)), jnp.tile(sin_t, (1, N_PAIR))


def _group_mean_matrix(width, group):
    idx = jnp.arange(width) // group
    return (idx[:, None] == idx[None, :]).astype(F32) / group


def _gla_chunk_matrices(rev):
    t = jnp.arange(TILE)
    same = (t[:, None] // GLA_CHUNK) == (t[None, :] // GLA_CHUNK)
    order = (t[None, :] >= t[:, None]) if rev else (t[None, :] <= t[:, None])
    return (same & order).astype(F32), same.astype(F32)


def kernel(x, c, ctx, c_ctx, w_ada, b_ada, norm1_g, norm2_g, w_in, ssm_lam_re, ssm_lam_im, ssm_log_step, ssm_b_re, ssm_b_im, ssm_c_re, ssm_c_im, ssm_d, ssm_w_glu, lru_conv_w, lru_conv_b, lru_wr, lru_br, lru_wi, lru_bi, lru_lam, da_q_norm, da_k_norm, da_lam, da_out_norm, gla_wa2, gla_ba, gla_out_norm, w_branch, w_out, w_up, ffn_conv_w, ffn_conv_b, w_down):
    depth = w_in.shape[0]
    n_lat, n_ctx = x.shape[1], ctx.shape[1]
    assert x.shape[0] == 1 and n_ctx == TILE and n_lat % TILE == 0
    xa = jnp.concatenate([ctx[0], x[0]], axis=0).astype(F32)

    cond = jnp.zeros((SUBLANES, D_MODEL), F32).at[0].set(c_ctx).at[1].set(c[0])
    ada = _adaln(cond, w_ada, b_ada)
    mods_all = jnp.pad(ada[:, :2].reshape(depth, 2, 6, D_MODEL), ((0, 0), (0, 0), (0, SUBLANES - 6), (0, 0)))

    cos_t, sin_t = _rope_tables(n_lat, n_ctx)
    grp32 = _group_mean_matrix(256, DA_HEAD)
    grp64 = _group_mean_matrix(256, GLA_DV)
    bd_mask = ((jnp.arange(W_GLA_K)[:, None] // GLA_DK) == (jnp.arange(256)[None, :] // GLA_DV)).astype(F32)
    gla_mats = [_gla_chunk_matrices(rev) for rev in (False, True)]
    tk = TILE * max(d for d in range(1, 9) if ((n_ctx + n_lat) // TILE) % d == 0)
    tq = min(512, n_lat)

    for l in range(depth):
        mods = mods_all[l]
        w_l = w_in[l]
        w_main = w_l[:, :W_MAIN].astype(BF16)
        w_a = jnp.pad(w_l[:, W_MAIN:W_MAIN + 2 * GLA_RANK], ((0, 0), (0, 128 - 2 * GLA_RANK))).astype(BF16)
        w_gates = w_l[:, W_MAIN + 2 * GLA_RANK:].astype(BF16)
        qg = jnp.tile(da_q_norm[l], N_PAIR)[None]
        kg = jnp.tile(da_k_norm[l], N_PAIR)[None]
        u, lx, ly, q, kt, v, gqk, gv, gg, ga, gates, qmax2, kmax2 = _inproj(
            xa, mods, norm1_g[l][None], w_main, w_a, w_gates, cos_t, sin_t, qg, kg, grp32)

        ys, hs, gs = [], [], []
        for di, rev in enumerate((False, True)):
            bcat, ccat, tabs = _s5_tables(ssm_lam_re[l, di], ssm_lam_im[l, di], ssm_log_step[l, di],
                                          ssm_b_re[l, di], ssm_b_im[l, di], ssm_c_re[l, di], ssm_c_im[l, di], rev)
            ys.append(_s5_scan(u, bcat, ccat, tabs, rev))
            hs.append(_lru_scan(lx, lru_conv_w[l], lru_conv_b[l][None], _block_diag(lru_wr[l, di]),
                                lru_br[l, di][None], _block_diag(lru_wi[l, di]), lru_bi[l, di][None],
                                lru_lam[l, di][None], rev))
            wa = jnp.zeros((128, W_GLA_K), F32).at[di * GLA_RANK:(di + 1) * GLA_RANK].set(gla_wa2[l, di])
            tri, sel = gla_mats[di]
            gs.append(_gla_scan(gqk, gv, ga, wa, gla_ba[l, di][None], tri, sel, bd_mask, rev))

        lp = da_lam[l].astype(F32)
        lam_init = 0.8 - 0.6 * math.exp(-0.3 * l)
        lam = jnp.exp(jnp.sum(lp[0] * lp[1])) - jnp.exp(jnp.sum(lp[2] * lp[3])) + lam_init
        qmax = jnp.sqrt(qmax2[0, ::DA_HEAD])
        kmax = jnp.sqrt(kmax2[0, ::DA_HEAD])
        scal = jnp.concatenate([lam.reshape(1), kmax, jnp.zeros((7,), F32)]).reshape(1, 16)

        def attend(online):
            att_lat = _attention(scal, q[:, n_ctx:], kt, v, tq, tk, online)
            att_ctx = _attention(scal, q[:, :n_ctx], kt[:, :n_ctx], v[:, :n_ctx], n_ctx, n_ctx, online)
            return att_ctx, att_lat

        safe = 2.0 * jnp.max(qmax * kmax) < MAX_SHIFT_SPREAD
        att_ctx, att_lat = lax.cond(safe, lambda: attend(False), lambda: attend(True))
        att = jnp.concatenate([att_ctx, att_lat], axis=1)

        xa = _merge(xa, mods, u, ys[0], ys[1], ssm_d[l][None], ssm_w_glu[l].astype(BF16), hs[0], hs[1], ly,
                    att, da_out_norm[l][None], gs[0], gs[1], gg, jnp.tile(gla_out_norm[l], GLA_HEADS)[None],
                    grp64, gates, w_branch[l].astype(BF16), w_out[l].astype(BF16), 1.0 - lam_init)
        xa = _ffn(xa, mods, norm2_g[l][None], w_up[l][:, :D_FF].astype(BF16), w_up[l][:, D_FF:].astype(BF16),
                  ffn_conv_w[l], ffn_conv_b[l][None], w_down[l].astype(BF16))
    return xa[n_ctx:][None].astype(x.dtype)
```

```python
import functools
import math

import jax
import jax.numpy as jnp
from jax import lax
from jax.experimental import pallas as pl
from jax.experimental.pallas import tpu as pltpu

F32 = jnp.float32
BF16 = jnp.bfloat16
HIGHEST = lax.Precision.HIGHEST

D_MODEL = 1024
EPS = 1e-6
GRID_W = 64
TILE = 256
SUBLANES = 8
HALO = SUBLANES
VMEM_LIMIT = 56 * 1024 * 1024

SSM_GROUPS, SSM_GROUP, SSM_STATE = 16, 16, 64
W_SSM = 256
N_STATE = SSM_GROUPS * SSM_STATE
W_LRU, LRU_BLOCKS, LRU_BLOCK, LRU_C = 256, 8, 32, 8.0
DA_HEADS, DA_HEAD, DA_VDIM = 4, 32, 64
N_PAIR = 2 * DA_HEADS
ROPE_BASE = 10000.0
GLA_HEADS, GLA_DK, GLA_DV, GLA_RANK, GLA_TAU, GLA_CHUNK = 4, 32, 64, 16, 16.0, 64
W_GLA_K = GLA_HEADS * GLA_DK
N_BRANCH, W_BRANCH, D_FF = 4, 256, 2816
W_MAIN = 2304
W_GATES = N_BRANCH * D_MODEL
Q_SCALE = DA_HEAD ** -0.5 * math.log2(math.e)
MAX_SHIFT_SPREAD = 100.0
ATT_TQ = 512
ATT_TK_TILES = 13
ATT_VROWS = 80


def _dot(a, b, precision=None):
    return jnp.dot(a, b, preferred_element_type=F32, precision=precision)


def _dot_nt(a, b, precision=None):
    return lax.dot_general(a, b, (((1,), (1,)), ((), ())), preferred_element_type=F32, precision=precision)


def _params(*sem):
    return pltpu.CompilerParams(dimension_semantics=sem, vmem_limit_bytes=VMEM_LIMIT)


def _const_spec(shape):
    nd = len(shape)
    return pl.BlockSpec(shape, lambda *_: (0,) * nd)


def _rms(x, g):
    return x * lax.rsqrt(jnp.mean(x * x, axis=-1, keepdims=True) + EPS) * g


def _adaln_kernel(c_ref, w_ref, b_ref, o_ref):
    c = c_ref[...]
    o_ref[0] = _dot(c * jax.nn.sigmoid(c), w_ref[0], HIGHEST) + b_ref[0]


def _adaln(cond, w_ada, b_ada):
    depth, _, n_out = w_ada.shape
    blk = 1536
    return pl.pallas_call(
        _adaln_kernel,
        grid=(depth, n_out // blk),
        in_specs=[pl.BlockSpec((SUBLANES, D_MODEL), lambda l, n: (0, 0)),
                  pl.BlockSpec((1, D_MODEL, blk), lambda l, n: (l, 0, n)),
                  pl.BlockSpec((1, 1, blk), lambda l, n: (l, 0, n))],
        out_specs=pl.BlockSpec((1, SUBLANES, blk), lambda l, n: (l, 0, n)),
        out_shape=jax.ShapeDtypeStruct((depth, SUBLANES, n_out), F32),
        compiler_params=_params("arbitrary", "arbitrary"),
        name="adaln",
    )(cond, w_ada, b_ada.reshape(depth, 1, n_out))


def _inproj_kernel(x_ref, mod_ref, g_ref, wm_ref, wa_ref, wg_ref, cos_ref, sin_ref, qg_ref, kg_ref, grp_ref,
                   u_ref, lx_ref, ly_ref, qt_ref, k_ref, vt_ref, gqk_ref, gv_ref, gg_ref, ga_ref, gates_ref,
                   qmax_ref, kmax_ref):
    x = x_ref[...]
    mod = mod_ref[0]
    hn = _rms(x, g_ref[...]) * (1.0 + mod[1:2]) + mod[0:1]
    hb = hn.astype(BF16)
    z = _dot(hb, wm_ref[...])
    u_ref[...] = z[:, 0:256]
    lx_ref[...] = z[:, 256:512]
    ly_ref[...] = z[:, 512:768]
    gqk_ref[...] = z[:, 1536:1792]
    gv_ref[...] = z[:, 1792:2048]
    gg_ref[...] = z[:, 2048:2304]
    ga_ref[...] = _dot(hb, wa_ref[...])
    gates_ref[...] = _dot(hb, wg_ref[...])

    cos, sin = cos_ref[...], sin_ref[...]
    lane = lax.broadcasted_iota(jnp.int32, (TILE, 256), 1)
    first_half = (lane % 16) < 8
    grp = grp_ref[...]

    def group_mean(sq):
        hi = sq.astype(BF16)
        lo = (sq - hi.astype(F32)).astype(BF16)
        return _dot(hi, grp) + _dot(lo, grp)

    def qk_norm_rope(t, g):
        tn = t * lax.rsqrt(group_mean(t * t) + EPS) * g
        partner = jnp.where(first_half, pltpu.roll(tn, 256 - 8, 1), pltpu.roll(tn, 8, 1))
        return tn * cos + partner * sin

    def max_sq_norm(tb):
        t = tb.astype(F32)
        return jnp.max(_dot((t * t).astype(BF16), grp) * DA_HEAD, axis=0, keepdims=True)

    @pl.when(pl.program_id(0) == 0)
    def _():
        qmax_ref[...] = jnp.zeros_like(qmax_ref)
        kmax_ref[...] = jnp.zeros_like(kmax_ref)

    qb = (qk_norm_rope(z[:, 768:1024], qg_ref[...]) * Q_SCALE).astype(BF16)
    kb = qk_norm_rope(z[:, 1024:1280], kg_ref[...]).astype(BF16)
    qmax_ref[...] = jnp.maximum(qmax_ref[...], max_sq_norm(qb))
    kmax_ref[...] = jnp.maximum(kmax_ref[...], max_sq_norm(kb))
    qt_ref[...] = qb.T
    for p in range(N_PAIR):
        k_ref[p] = kb[:, p * DA_HEAD:(p + 1) * DA_HEAD]
    lane_v = lax.broadcasted_iota(jnp.int32, (TILE, 128), 1)
    tail = (lane_v == DA_VDIM).astype(F32)
    for h in range(DA_HEADS):
        vh = z[:, 1280 + 128 * (h // 2):1280 + 128 * (h // 2 + 1)]
        if h % 2:
            vh = pltpu.roll(vh, DA_VDIM, 1)
        vt_ref[h] = jnp.where(lane_v < DA_VDIM, vh, tail).T[:ATT_VROWS].astype(BF16)


def _inproj(xa, mods, norm_g, w_main, w_a, w_gates, cos_t, sin_t, qg, kg, grp32):
    t_all = xa.shape[0]
    nt = t_all // TILE
    row = lambda w: pl.BlockSpec((TILE, w), lambda i: (i, 0))
    f = lambda w: jax.ShapeDtypeStruct((t_all, w), F32)
    return pl.pallas_call(
        _inproj_kernel,
        grid=(nt,),
        in_specs=[row(D_MODEL),
                  pl.BlockSpec((1, SUBLANES, D_MODEL), lambda i: (jnp.minimum(i, 1), 0, 0)),
                  _const_spec((1, D_MODEL)),
                  _const_spec((D_MODEL, W_MAIN)), _const_spec((D_MODEL, 128)), _const_spec((D_MODEL, W_GATES)),
                  row(256), row(256), _const_spec((1, 256)), _const_spec((1, 256)), _const_spec((256, 256))],
        out_specs=[row(256), row(256), row(256),
                   pl.BlockSpec((256, TILE), lambda i: (0, i)),
                   pl.BlockSpec((N_PAIR, TILE, DA_HEAD), lambda i: (0, i, 0)),
                   pl.BlockSpec((DA_HEADS, ATT_VROWS, TILE), lambda i: (0, 0, i)),
                   row(256), row(256), row(256), row(128), row(W_GATES),
                   _const_spec((SUBLANES, 256)), _const_spec((SUBLANES, 256))],
        out_shape=[f(256), f(256), f(256), jax.ShapeDtypeStruct((256, t_all), BF16),
                   jax.ShapeDtypeStruct((N_PAIR, t_all, DA_HEAD), BF16),
                   jax.ShapeDtypeStruct((DA_HEADS, ATT_VROWS, t_all), BF16),
                   f(256), f(256), f(256), f(128), f(W_GATES),
                   jax.ShapeDtypeStruct((SUBLANES, 256), F32), jax.ShapeDtypeStruct((SUBLANES, 256), F32)],
        compiler_params=_params("arbitrary"),
        name="inproj",
    )(xa, mods, norm_g, w_main, w_a, w_gates, cos_t, sin_t, qg, kg, grp32)


def _scan_tile_index(j, nt, rev):
    return jnp.where(j == 0, 0, nt - j) if rev else j


def _s5_kernel(u_ref, b_ref, c_ref, tab_ref, y_ref, hre, him, car, *, rev):
    j = pl.program_id(0)

    @pl.when(j == 0)
    def _():
        car[...] = jnp.zeros_like(car)

    bu = _dot(u_ref[...].astype(BF16), b_ref[...])
    hre[...] = bu[:, :N_STATE]
    him[...] = bu[:, N_STATE:]
    nblk = TILE // SUBLANES

    def body(b, carry):
        cr, ci = carry
        blk = nblk - 1 - b if rev else b
        r0 = pl.multiple_of(blk * SUBLANES, SUBLANES)
        xr = hre[pl.ds(r0, SUBLANES), :]
        xi = him[pl.ds(r0, SUBLANES), :]
        for k, s in enumerate((1, 2, 4)):
            shift = SUBLANES - s if rev else s
            sr = pltpu.roll(xr, shift, 0)
            si = pltpu.roll(xi, shift, 0)
            mr, mi = tab_ref[2 * k], tab_ref[2 * k + 1]
            xr, xi = xr + mr * sr - mi * si, xi + mr * si + mi * sr
        pr, pi_ = tab_ref[6], tab_ref[7]
        xr, xi = xr + pr * cr - pi_ * ci, xi + pr * ci + pi_ * cr
        hre[pl.ds(r0, SUBLANES), :] = xr
        him[pl.ds(r0, SUBLANES), :] = xi
        last = 0 if rev else SUBLANES - 1
        return (jnp.broadcast_to(xr[last:last + 1], xr.shape), jnp.broadcast_to(xi[last:last + 1], xi.shape))

    cr, ci = lax.fori_loop(0, nblk, body, (car[0], car[1]))
    car[0] = cr
    car[1] = ci
    y_ref[...] = (_dot(hre[...].astype(BF16), c_ref[:N_STATE]) + _dot(him[...].astype(BF16), c_ref[N_STATE:]))


def _s5_scan(u, bcat, ccat, tabs, rev):
    t_all = u.shape[0]
    nt = t_all // TILE
    tix = lambda j: (_scan_tile_index(j, nt, rev), 0)
    return pl.pallas_call(
        functools.partial(_s5_kernel, rev=rev),
        grid=(nt,),
        in_specs=[pl.BlockSpec((TILE, W_SSM), tix), _const_spec((W_SSM, 2 * N_STATE)),
                  _const_spec((2 * N_STATE, W_SSM)), _const_spec((8, SUBLANES, N_STATE))],
        out_specs=pl.BlockSpec((TILE, W_SSM), tix),
        out_shape=jax.ShapeDtypeStruct((t_all, W_SSM), F32),
        scratch_shapes=[pltpu.VMEM((TILE, N_STATE), F32), pltpu.VMEM((TILE, N_STATE), F32),
                        pltpu.VMEM((2, SUBLANES, N_STATE), F32)],
        compiler_params=_params("arbitrary"),
        name="s5_rev" if rev else "s5_fwd",
    )(u, bcat, ccat, tabs)


def _s5_tables(lam_re, lam_im, log_step, b_re, b_im, c_re, c_im, rev):
    dt = jnp.exp(log_step.astype(F32))[:, None]
    lre, lim = lam_re.astype(F32) * dt, lam_im.astype(F32) * dt
    mag = jnp.exp(lre)
    lb_re, lb_im = mag * jnp.cos(lim), mag * jnp.sin(lim)
    num_re, num_im = lb_re - 1.0, lb_im
    den = lam_re * lam_re + lam_im * lam_im
    k_re = (num_re * lam_re + num_im * lam_im) / den
    k_im = (num_im * lam_re - num_re * lam_im) / den
    bb_re = k_re[..., None] * b_re - k_im[..., None] * b_im
    bb_im = k_re[..., None] * b_im + k_im[..., None] * b_re
    eye = jnp.eye(SSM_GROUPS, dtype=F32)
    blk_in = lambda t: jnp.einsum('gpc,gh->gchp', t, eye).reshape(W_SSM, N_STATE)
    blk_out = lambda t: jnp.einsum('gcp,gh->gphc', t, eye).reshape(N_STATE, W_SSM)
    bcat = jnp.concatenate([blk_in(bb_re), blk_in(bb_im)], axis=1).astype(BF16)
    ccat = jnp.concatenate([blk_out(c_re.astype(F32)), -blk_out(c_im.astype(F32))], axis=0).astype(BF16)

    def power(k):
        k = k[:, None].astype(F32)
        m = jnp.exp(k * lre.reshape(1, -1))
        return m * jnp.cos(k * lim.reshape(1, -1)), m * jnp.sin(k * lim.reshape(1, -1))

    r = jnp.arange(SUBLANES)
    tabs = []
    for s in (1, 2, 4):
        pr, pi_ = power(jnp.full((SUBLANES,), s))
        keep = ((r + s <= SUBLANES - 1) if rev else (r >= s))[:, None]
        tabs += [jnp.where(keep, pr, 0.0), jnp.where(keep, pi_, 0.0)]
    tabs += list(power((SUBLANES - r) if rev else (r + 1)))
    return bcat, ccat, jnp.stack(tabs)


def _lru_kernel(x_ref, xp_ref, xn_ref, cw_ref, cb_ref, wr_ref, br_ref, wi_ref, bi_ref, lam_ref,
                h_ref, ext, a_sc, b_sc, car, *, rev, nt):
    j = pl.program_id(0)
    ti = _scan_tile_index(j, nt, rev)

    @pl.when(j == 0)
    def _():
        car[...] = jnp.zeros_like(car)

    prev_ok = (ti >= 2).astype(F32)
    next_ok = jnp.logical_and(ti != 0, ti != nt - 1).astype(F32)
    ext[0:HALO] = xp_ref[...] * prev_ok
    ext[HALO:HALO + TILE] = x_ref[...]
    ext[HALO + TILE:] = xn_ref[...] * next_ok
    xc = cb_ref[...] + sum(cw_ref[k:k + 1] * ext[pl.ds(HALO - 2 + k, TILE), :] for k in range(4))
    xb = xc.astype(BF16)
    r = jax.nn.sigmoid(_dot(xb, wr_ref[...]) + br_ref[...])
    i = jax.nn.sigmoid(_dot(xb, wi_ref[...]) + bi_ref[...])
    log_a = LRU_C * r * jax.nn.log_sigmoid(lam_ref[...])
    a = jnp.exp(log_a)
    a_sc[...] = a
    b_sc[...] = jnp.sqrt(-jnp.tanh(log_a) * (a * a + 1.0)) * (i * xc)
    nblk = TILE // SUBLANES
    rows = lax.broadcasted_iota(jnp.int32, (SUBLANES, W_LRU), 0)

    def body(b, carry):
        blk = nblk - 1 - b if rev else b
        r0 = pl.multiple_of(blk * SUBLANES, SUBLANES)
        a = a_sc[pl.ds(r0, SUBLANES), :]
        x = b_sc[pl.ds(r0, SUBLANES), :]
        for s in (1, 2, 4):
            shift = SUBLANES - s if rev else s
            keep = (rows + s <= SUBLANES - 1) if rev else (rows >= s)
            a_s = jnp.where(keep, pltpu.roll(a, shift, 0), 1.0)
            x_s = jnp.where(keep, pltpu.roll(x, shift, 0), 0.0)
            a, x = a * a_s, a * x_s + x
        h = x + a * carry
        b_sc[pl.ds(r0, SUBLANES), :] = h
        last = 0 if rev else SUBLANES - 1
        return jnp.broadcast_to(h[last:last + 1], h.shape)

    car[...] = lax.fori_loop(0, nblk, body, car[...])
    h_ref[...] = b_sc[...]


def _lru_scan(lx, conv_w, conv_b, wr, br, wi, bi, lam, rev):
    t_all = lx.shape[0]
    nt = t_all // TILE
    per = TILE // HALO
    tix = lambda j: (_scan_tile_index(j, nt, rev), 0)
    pix = lambda j: (jnp.maximum(_scan_tile_index(j, nt, rev) * per - 1, 0), 0)
    nix = lambda j: (jnp.minimum((_scan_tile_index(j, nt, rev) + 1) * per, nt * per - 1), 0)
    vec = _const_spec((1, W_LRU))
    return pl.pallas_call(
        functools.partial(_lru_kernel, rev=rev, nt=nt),
        grid=(nt,),
        in_specs=[pl.BlockSpec((TILE, W_LRU), tix), pl.BlockSpec((HALO, W_LRU), pix), pl.BlockSpec((HALO, W_LRU), nix),
                  _const_spec((4, W_LRU)), vec, _const_spec((W_LRU, W_LRU)), vec, _const_spec((W_LRU, W_LRU)), vec, vec],
        out_specs=pl.BlockSpec((TILE, W_LRU), tix),
        out_shape=jax.ShapeDtypeStruct((t_all, W_LRU), F32),
        scratch_shapes=[pltpu.VMEM((TILE + 2 * HALO, W_LRU), F32), pltpu.VMEM((TILE, W_LRU), F32),
                        pltpu.VMEM((TILE, W_LRU), F32), pltpu.VMEM((SUBLANES, W_LRU), F32)],
        compiler_params=_params("arbitrary"),
        name="lru_rev" if rev else "lru_fwd",
    )(lx, lx, lx, conv_w, conv_b, wr, br, wi, bi, lam)


def _block_diag(w):
    n, a, b = w.shape
    return jnp.einsum('nab,nm->namb', w, jnp.eye(n, dtype=w.dtype)).reshape(n * a, n * b)


def _attn_kernel(sc_ref, qt_ref, k_ref, vt_ref, o_ref, m_sc, acc_sc, *, nkv, online):
    j = pl.program_id(1)

    @pl.when(j == 0)
    def _():
        acc_sc[...] = jnp.zeros_like(acc_sc)
        for p in range(N_PAIR):
            if online:
                m_sc[p] = jnp.full(m_sc.shape[1:], -jnp.inf, F32)
            else:
                qf = qt_ref[p * DA_HEAD:(p + 1) * DA_HEAD, :].astype(F32)
                m_sc[p] = jnp.sqrt(jnp.sum(qf * qf, axis=0, keepdims=True)) * sc_ref[0, 1 + p]

    for p in range(N_PAIR):
        s = _dot(k_ref[p], qt_ref[p * DA_HEAD:(p + 1) * DA_HEAD, :])
        if online:
            m_old = m_sc[p]
            m_new = jnp.maximum(m_old, jnp.max(s, axis=0, keepdims=True))
            m_sc[p] = m_new
            acc_sc[p] = jnp.exp2(m_old - m_new) * acc_sc[p] + _dot(vt_ref[p // 2], jnp.exp2(s - m_new).astype(BF16))
        else:
            acc_sc[p] += _dot(vt_ref[p // 2], jnp.exp2(s - m_sc[p]).astype(BF16))

    @pl.when(j == nkv - 1)
    def _():
        lam = sc_ref[0, 0]
        for h in range(DA_HEADS):
            a0, a1 = acc_sc[2 * h], acc_sc[2 * h + 1]
            o_t = (a0[:DA_VDIM] / a0[DA_VDIM:DA_VDIM + 1] - lam * (a1[:DA_VDIM] / a1[DA_VDIM:DA_VDIM + 1]))
            o_ref[h] = o_t.T


def _attention(scal, qt, k, vt, tq, tk, online):
    n_q = qt.shape[1]
    nq, nkv = n_q // tq, k.shape[1] // tk
    return pl.pallas_call(
        functools.partial(_attn_kernel, nkv=nkv, online=online),
        grid=(nq, nkv),
        in_specs=[pl.BlockSpec(memory_space=pltpu.SMEM),
                  pl.BlockSpec((256, tq), lambda i, j: (0, i)),
                  pl.BlockSpec((N_PAIR, tk, DA_HEAD), lambda i, j: (0, j, 0)),
                  pl.BlockSpec((DA_HEADS, ATT_VROWS, tk), lambda i, j: (0, 0, j))],
        out_specs=pl.BlockSpec((DA_HEADS, tq, DA_VDIM), lambda i, j: (0, i, 0)),
        out_shape=jax.ShapeDtypeStruct((DA_HEADS, n_q, DA_VDIM), F32),
        scratch_shapes=[pltpu.VMEM((N_PAIR, 1, tq), F32), pltpu.VMEM((N_PAIR, ATT_VROWS, tq), F32)],
        compiler_params=_params("arbitrary", "arbitrary"),
        name="diff_attn_online" if online else "diff_attn",
    )(scal, qt, k, vt)


def _gla_kernel(qk_ref, v_ref, ga_ref, wa_ref, ba_ref, tri_ref, sel_ref, bd_ref, o_ref, s_sc, *, rev):
    j = pl.program_id(0)

    @pl.when(j == 0)
    def _():
        s_sc[...] = jnp.zeros_like(s_sc)

    q = qk_ref[:, :W_GLA_K] * (GLA_DK ** -0.5)
    k = qk_ref[:, W_GLA_K:]
    v = v_ref[...]
    lg = jax.nn.log_sigmoid(_dot(ga_ref[...], wa_ref[...], HIGHEST) + ba_ref[...]) / GLA_TAU
    lg_hi = lg.astype(BF16)
    lg_lo = (lg - lg_hi.astype(F32)).astype(BF16)
    b = _dot(tri_ref[...], lg_hi) + _dot(tri_ref[...], lg_lo)
    b_last = _dot(sel_ref[...], lg_hi) + _dot(sel_ref[...], lg_lo)
    qc = q * jnp.exp(b)
    kinv = k * jnp.exp(-b)
    kdec_t = (k * jnp.exp(b_last - b)).T
    decay_t = jnp.exp(b_last).T
    bd = bd_ref[...]
    ii = lax.broadcasted_iota(jnp.int32, (GLA_CHUNK, GLA_CHUNK), 0)
    jj = lax.broadcasted_iota(jnp.int32, (GLA_CHUNK, GLA_CHUNK), 1)
    causal = (jj >= ii) if rev else (jj <= ii)
    lane_k = lax.broadcasted_iota(jnp.int32, (GLA_CHUNK, W_GLA_K), 1) // GLA_DK
    lane_v = lax.broadcasted_iota(jnp.int32, (GLA_CHUNK, 256), 1) // GLA_DV
    nchunk = TILE // GLA_CHUNK
    s_state = s_sc[...]
    for ci in range(nchunk):
        c = nchunk - 1 - ci if rev else ci
        rs = slice(c * GLA_CHUNK, (c + 1) * GLA_CHUNK)
        qcc, kic, vc = qc[rs], kinv[rs], v[rs]
        o = _dot(qcc, s_state)
        for h in range(GLA_HEADS):
            att = _dot_nt(jnp.where(lane_k == h, qcc, 0.0), kic)
            att = jnp.where(causal, att, 0.0)
            o = o + jnp.where(lane_v == h, _dot(att, vc), 0.0)
        o_ref[rs, :] = o
        kv = _dot(kdec_t[:, rs], vc) * bd
        s_state = decay_t[:, c * GLA_CHUNK:c * GLA_CHUNK + 1] * s_state + kv
    s_sc[...] = s_state


def _gla_scan(gqk, gv, ga, wa, ba, tri, sel, bd, rev):
    t_all = gqk.shape[0]
    nt = t_all // TILE
    tix = lambda j: (_scan_tile_index(j, nt, rev), 0)
    return pl.pallas_call(
        functools.partial(_gla_kernel, rev=rev),
        grid=(nt,),
        in_specs=[pl.BlockSpec((TILE, 256), tix), pl.BlockSpec((TILE, 256), tix), pl.BlockSpec((TILE, 128), tix),
                  _const_spec((128, W_GLA_K)), _const_spec((1, W_GLA_K)),
                  _const_spec((TILE, TILE)), _const_spec((TILE, TILE)), _const_spec((W_GLA_K, 256))],
        out_specs=pl.BlockSpec((TILE, 256), tix),
        out_shape=jax.ShapeDtypeStruct((t_all, 256), F32),
        scratch_shapes=[pltpu.VMEM((W_GLA_K, 256), F32)],
        compiler_params=_params("arbitrary"),
        name="gla_rev" if rev else "gla_fwd",
    )(gqk, gv, ga, wa, ba, tri, sel, bd)


def _merge_kernel(x_ref, mod_ref, u_ref, yf_ref, yb_ref, d_ref, wglu_ref, hf_ref, hb_ref, ly_ref,
                  att_ref, dan_ref, gf_ref, gb_ref, gg_ref, gn_ref, grp_ref, gates_ref, wb_ref, wo_ref,
                  o_ref, *, da_scale):
    z = jax.nn.gelu(u_ref[...] * d_ref[...] + yf_ref[...] + yb_ref[...])
    ya = z * jax.nn.sigmoid(_dot(z.astype(BF16), wglu_ref[...]))
    yb = (hf_ref[...] + hb_ref[...]) * jax.nn.gelu(ly_ref[...])
    og = gf_ref[...] + gb_ref[...]
    sq = og * og
    sq_hi = sq.astype(BF16)
    ms = _dot(sq_hi, grp_ref[...]) + _dot((sq - sq_hi.astype(F32)).astype(BF16), grp_ref[...])
    gg = gg_ref[...]
    yd = og * lax.rsqrt(ms + EPS) * gn_ref[...] * (gg * jax.nn.sigmoid(gg))

    def gate(n):
        return jax.nn.sigmoid(gates_ref[:, n * D_MODEL:(n + 1) * D_MODEL])

    acc = gate(0) * _dot(ya.astype(BF16), wb_ref[0])
    acc = acc + gate(1) * _dot(yb.astype(BF16), wb_ref[1])
    pc = jnp.zeros((TILE, D_MODEL), F32)
    for h in range(DA_HEADS):
        yc = _rms(att_ref[h], dan_ref[...]) * da_scale
        pc = pc + _dot(yc.astype(BF16), wb_ref[2, h * DA_VDIM:(h + 1) * DA_VDIM, :])
    acc = acc + gate(2) * pc
    acc = acc + gate(3) * _dot(yd.astype(BF16), wb_ref[3])
    o_ref[...] = x_ref[...] + mod_ref[0, 2:3] * _dot(acc.astype(BF16), wo_ref[...])


def _merge(xa, mods, u, yf, yb, ssm_d, w_glu, hf, hb, ly, att, da_norm, gf, gb, gg, gla_norm, grp64, gates,
           w_branch, w_out, da_scale):
    t_all = xa.shape[0]
    nt = t_all // TILE
    row = lambda w: pl.BlockSpec((TILE, w), lambda i: (i, 0))
    vec = lambda w: _const_spec((1, w))
    return pl.pallas_call(
        functools.partial(_merge_kernel, da_scale=da_scale),
        grid=(nt,),
        in_specs=[row(D_MODEL), pl.BlockSpec((1, SUBLANES, D_MODEL), lambda i: (jnp.minimum(i, 1), 0, 0)),
                  row(256), row(256), row(256), vec(256), _const_spec((256, 256)),
                  row(256), row(256), row(256),
                  pl.BlockSpec((DA_HEADS, TILE, DA_VDIM), lambda i: (0, i, 0)), vec(DA_VDIM),
                  row(256), row(256), row(256), vec(256), _const_spec((256, 256)), row(W_GATES),
                  _const_spec((N_BRANCH, W_BRANCH, D_MODEL)), _const_spec((D_MODEL, D_MODEL))],
        out_specs=row(D_MODEL),
        out_shape=jax.ShapeDtypeStruct((t_all, D_MODEL), F32),
        compiler_params=_params("arbitrary"),
        name="merge",
    )(xa, mods, u, yf, yb, ssm_d, w_glu, hf, hb, ly, att, da_norm, gf, gb, gg, gla_norm, grp64, gates,
      w_branch, w_out)


def _ffn_kernel(x_ref, xp_ref, xn_ref, mod_ref, g_ref, wa_ref, wg_ref, cw_ref, cb_ref, wd_ref, o_ref, ext, gt_sc,
                *, nt):
    i = pl.program_id(0)
    mod = mod_ref[0]
    g = g_ref[...]
    norm = lambda t: _rms(t, g) * (1.0 + mod[4:5]) + mod[3:4]
    prev_ok = (i >= 2).astype(F32)
    next_ok = jnp.logical_and(i != 0, i != nt - 1).astype(F32)
    x = x_ref[...]
    hn = norm(x)
    ext[0:HALO] = norm(xp_ref[...]) * prev_ok
    ext[HALO:HALO + TILE] = hn
    ext[HALO + TILE:] = norm(xn_ref[...]) * next_ok
    gt_sc[...] = _dot(ext[...].astype(BF16), wg_ref[...])
    gt = cb_ref[...] + sum(cw_ref[k:k + 1] * gt_sc[pl.ds(HALO - 1 + k, TILE), :] for k in range(3))
    a = _dot(hn.astype(BF16), wa_ref[...])
    y = _dot((jax.nn.gelu(gt) * a).astype(BF16), wd_ref[...])
    o_ref[...] = x + mod[5:6] * y


def _ffn(xa, mods, norm_g, w_up_a, w_up_g, conv_w, conv_b, w_down):
    t_all = xa.shape[0]
    nt = t_all // TILE
    per = TILE // HALO
    return pl.pallas_call(
        functools.partial(_ffn_kernel, nt=nt),
        grid=(nt,),
        in_specs=[pl.BlockSpec((TILE, D_MODEL), lambda i: (i, 0)),
                  pl.BlockSpec((HALO, D_MODEL), lambda i: (jnp.maximum(i * per - 1, 0), 0)),
                  pl.BlockSpec((HALO, D_MODEL), lambda i: (jnp.minimum((i + 1) * per, nt * per - 1), 0)),
                  pl.BlockSpec((1, SUBLANES, D_MODEL), lambda i: (jnp.minimum(i, 1), 0, 0)),
                  _const_spec((1, D_MODEL)), _const_spec((D_MODEL, D_FF)), _const_spec((D_MODEL, D_FF)),
                  _const_spec((3, D_FF)), _const_spec((1, D_FF)), _const_spec((D_FF, D_MODEL))],
        out_specs=pl.BlockSpec((TILE, D_MODEL), lambda i: (i, 0)),
        out_shape=jax.ShapeDtypeStruct((t_all, D_MODEL), F32),
        scratch_shapes=[pltpu.VMEM((TILE + 2 * HALO, D_MODEL), F32), pltpu.VMEM((TILE + 2 * HALO, D_FF), F32)],
        compiler_params=_params("arbitrary"),
        name="convffn",
    )(xa, xa, xa, mods, norm_g, w_up_a, w_up_g, conv_w, conv_b, w_down)


def _rope_tables(n_lat, n_ctx):
    pos = jnp.arange(n_lat)
    row, col = (pos // GRID_W).astype(F32), (pos % GRID_W).astype(F32)
    half = DA_HEAD // 2
    inv = 1.0 / (ROPE_BASE ** (jnp.arange(0, half, 2, dtype=F32) / half))
    ang = jnp.stack([row[:, None] * inv, col[:, None] * inv], axis=1)
    cos, sin = jnp.cos(ang), jnp.sin(ang)
    cos32 = jnp.stack([cos, cos], axis=2).reshape(n_lat, DA_HEAD)
    sin32 = jnp.stack([-sin, sin], axis=2).reshape(n_lat, DA_HEAD)
    cos_t = jnp.concatenate([jnp.ones((n_ctx, DA_HEAD), F32), cos32], axis=0)
    sin_t = jnp.concatenate([jnp.zeros((n_ctx, DA_HEAD), F32), sin32], axis=0)
    return jnp.tile(cos_t, (1, N_PAIR)), jnp.tile(sin_t, (1, N_PAIR))


def _group_mean_matrix(width, group):
    idx = jnp.arange(width) // group
    return ((idx[:, None] == idx[None, :]).astype(F32) / group).astype(BF16)


def _gla_chunk_matrices(rev):
    t = jnp.arange(TILE)
    same = (t[:, None] // GLA_CHUNK) == (t[None, :] // GLA_CHUNK)
    order = (t[None, :] >= t[:, None]) if rev else (t[None, :] <= t[:, None])
    return (same & order).astype(BF16), same.astype(BF16)


def kernel(x, c, ctx, c_ctx, w_ada, b_ada, norm1_g, norm2_g, w_in, ssm_lam_re, ssm_lam_im, ssm_log_step, ssm_b_re, ssm_b_im, ssm_c_re, ssm_c_im, ssm_d, ssm_w_glu, lru_conv_w, lru_conv_b, lru_wr, lru_br, lru_wi, lru_bi, lru_lam, da_q_norm, da_k_norm, da_lam, da_out_norm, gla_wa2, gla_ba, gla_out_norm, w_branch, w_out, w_up, ffn_conv_w, ffn_conv_b, w_down):
    depth = w_in.shape[0]
    n_lat, n_ctx = x.shape[1], ctx.shape[1]
    assert x.shape[0] == 1 and n_ctx == TILE and n_lat % TILE == 0
    xa = jnp.concatenate([ctx[0], x[0]], axis=0).astype(F32)

    cond = jnp.zeros((SUBLANES, D_MODEL), F32).at[0].set(c_ctx).at[1].set(c[0])
    ada = _adaln(cond, w_ada, b_ada)
    mods_all = jnp.pad(ada[:, :2].reshape(depth, 2, 6, D_MODEL), ((0, 0), (0, 0), (0, SUBLANES - 6), (0, 0)))

    cos_t, sin_t = _rope_tables(n_lat, n_ctx)
    grp32 = _group_mean_matrix(256, DA_HEAD)
    grp64 = _group_mean_matrix(256, GLA_DV)
    bd_mask = ((jnp.arange(W_GLA_K)[:, None] // GLA_DK) == (jnp.arange(256)[None, :] // GLA_DV)).astype(F32)
    gla_mats = [_gla_chunk_matrices(rev) for rev in (False, True)]
    tk = TILE * max(d for d in range(1, ATT_TK_TILES + 1) if ((n_ctx + n_lat) // TILE) % d == 0)
    tq = min(ATT_TQ, n_lat)

    for l in range(depth):
        mods = mods_all[l]
        w_l = w_in[l]
        w_main = w_l[:, :W_MAIN].astype(BF16)
        w_a = jnp.pad(w_l[:, W_MAIN:W_MAIN + 2 * GLA_RANK], ((0, 0), (0, 128 - 2 * GLA_RANK))).astype(BF16)
        w_gates = w_l[:, W_MAIN + 2 * GLA_RANK:].astype(BF16)
        qg = jnp.tile(da_q_norm[l], N_PAIR)[None]
        kg = jnp.tile(da_k_norm[l], N_PAIR)[None]
        u, lx, ly, qt, k, vt, gqk, gv, gg, ga, gates, qmax2, kmax2 = _inproj(
            xa, mods, norm1_g[l][None], w_main, w_a, w_gates, cos_t, sin_t, qg, kg, grp32)

        ys, hs, gs = [], [], []
        for di, rev in enumerate((False, True)):
            bcat, ccat, tabs = _s5_tables(ssm_lam_re[l, di], ssm_lam_im[l, di], ssm_log_step[l, di],
                                          ssm_b_re[l, di], ssm_b_im[l, di], ssm_c_re[l, di], ssm_c_im[l, di], rev)
            ys.append(_s5_scan(u, bcat, ccat, tabs, rev))
            hs.append(_lru_scan(lx, lru_conv_w[l], lru_conv_b[l][None], _block_diag(lru_wr[l, di]).astype(BF16),
                                lru_br[l, di][None], _block_diag(lru_wi[l, di]).astype(BF16), lru_bi[l, di][None],
                                lru_lam[l, di][None], rev))
            wa = jnp.zeros((128, W_GLA_K), F32).at[di * GLA_RANK:(di + 1) * GLA_RANK].set(gla_wa2[l, di])
            tri, sel = gla_mats[di]
            gs.append(_gla_scan(gqk, gv, ga, wa, gla_ba[l, di][None], tri, sel, bd_mask, rev))

        lp = da_lam[l].astype(F32)
        lam_init = 0.8 - 0.6 * math.exp(-0.3 * l)
        lam = jnp.exp(jnp.sum(lp[0] * lp[1])) - jnp.exp(jnp.sum(lp[2] * lp[3])) + lam_init
        qmax = jnp.sqrt(qmax2[0, ::DA_HEAD])
        kmax = jnp.sqrt(kmax2[0, ::DA_HEAD])
        scal = jnp.concatenate([lam.reshape(1), kmax, jnp.zeros((7,), F32)]).reshape(1, 16)

        def attend(online):
            att_lat = _attention(scal, qt[:, n_ctx:], k, vt, tq, tk, online)
            att_ctx = _attention(scal, qt[:, :n_ctx], k[:, :n_ctx], vt[:, :, :n_ctx], n_ctx, n_ctx, online)
            return att_ctx, att_lat

        safe = 2.0 * jnp.max(qmax * kmax) < MAX_SHIFT_SPREAD
        att_ctx, att_lat = lax.cond(safe, lambda: attend(False), lambda: attend(True))
        att = jnp.concatenate([att_ctx, att_lat], axis=1)

        xa = _merge(xa, mods, u, ys[0], ys[1], ssm_d[l][None], ssm_w_glu[l].astype(BF16), hs[0], hs[1], ly,
                    att, da_out_norm[l][None], gs[0], gs[1], gg, jnp.tile(gla_out_norm[l], GLA_HEADS)[None],
                    grp64, gates, w_branch[l].astype(BF16), w_out[l].astype(BF16), 1.0 - lam_init)
        xa = _ffn(xa, mods, norm2_g[l][None], w_up[l][:, :D_FF].astype(BF16), w_up[l][:, D_FF:].astype(BF16),
                  ffn_conv_w[l], ffn_conv_b[l][None], w_down[l].astype(BF16))
    return xa[n_ctx:][None].astype(x.dtype)
```

```python
import functools
import math

import jax
import jax.numpy as jnp
from jax import lax
from jax.experimental import pallas as pl
from jax.experimental.pallas import tpu as pltpu

F32 = jnp.float32
BF16 = jnp.bfloat16
HIGHEST = lax.Precision.HIGHEST

D_MODEL = 1024
EPS = 1e-6
GRID_W = 64
TILE = 256
SUBLANES = 8
HALO = SUBLANES
VMEM_LIMIT = 56 * 1024 * 1024

SSM_GROUPS, SSM_GROUP, SSM_STATE = 16, 16, 64
W_SSM = 256
N_STATE = SSM_GROUPS * SSM_STATE
W_LRU, LRU_BLOCKS, LRU_BLOCK, LRU_C = 256, 8, 32, 8.0
DA_HEADS, DA_HEAD, DA_VDIM = 4, 32, 64
N_PAIR = 2 * DA_HEADS
ROPE_BASE = 10000.0
GLA_HEADS, GLA_DK, GLA_DV, GLA_RANK, GLA_TAU, GLA_CHUNK = 4, 32, 64, 16, 16.0, 64
W_GLA_K = GLA_HEADS * GLA_DK
N_BRANCH, W_BRANCH, D_FF = 4, 256, 2816
W_MAIN = 2304
W_GATES = N_BRANCH * D_MODEL
Q_SCALE = DA_HEAD ** -0.5 * math.log2(math.e)
MAX_SHIFT_SPREAD = 100.0
ATT_TQ = 512
ATT_TK_TILES = 13
ATT_VROWS = 80


def _dot(a, b, precision=None):
    return jnp.dot(a, b, preferred_element_type=F32, precision=precision)


def _dot_nt(a, b, precision=None):
    return lax.dot_general(a, b, (((1,), (1,)), ((), ())), preferred_element_type=F32, precision=precision)


def _params(*sem):
    return pltpu.CompilerParams(dimension_semantics=sem, vmem_limit_bytes=VMEM_LIMIT)


def _const_spec(shape):
    nd = len(shape)
    return pl.BlockSpec(shape, lambda *_: (0,) * nd)


def _rms(x, g):
    return x * lax.rsqrt(jnp.mean(x * x, axis=-1, keepdims=True) + EPS) * g


def _group_mean(sq, grp):
    hi = sq.astype(BF16)
    lo = (sq - hi.astype(F32)).astype(BF16)
    return _dot(hi, grp) + _dot(lo, grp)


def _adaln_kernel(c_ref, w_ref, b_ref, o_ref):
    c = c_ref[...]
    o_ref[0] = _dot(c * jax.nn.sigmoid(c), w_ref[0], HIGHEST) + b_ref[0]


def _adaln(cond, w_ada, b_ada):
    depth, _, n_out = w_ada.shape
    blk = 1536
    return pl.pallas_call(
        _adaln_kernel,
        grid=(depth, n_out // blk),
        in_specs=[pl.BlockSpec((SUBLANES, D_MODEL), lambda l, n: (0, 0)),
                  pl.BlockSpec((1, D_MODEL, blk), lambda l, n: (l, 0, n)),
                  pl.BlockSpec((1, 1, blk), lambda l, n: (l, 0, n))],
        out_specs=pl.BlockSpec((1, SUBLANES, blk), lambda l, n: (l, 0, n)),
        out_shape=jax.ShapeDtypeStruct((depth, SUBLANES, n_out), F32),
        compiler_params=_params("arbitrary", "arbitrary"),
        name="adaln",
    )(cond, w_ada, b_ada.reshape(depth, 1, n_out))


def _inproj_kernel(x_ref, mod_ref, g_ref, wm_ref, wa_ref, wg_ref, cos_ref, sin_ref, qg_ref, kg_ref, grp_ref,
                   u_ref, lx_ref, ly_ref, qt_ref, k_ref, vt_ref, gqk_ref, gv_ref, gg_ref, ga_ref, gates_ref,
                   qmax_ref, kmax_ref):
    x = x_ref[...]
    mod = mod_ref[0]
    hn = _rms(x, g_ref[...]) * (1.0 + mod[1:2]) + mod[0:1]
    hb = hn.astype(BF16)
    z = _dot(hb, wm_ref[...])
    u_ref[...] = z[:, 0:256]
    lx_ref[...] = z[:, 256:512]
    ly_ref[...] = z[:, 512:768]
    gqk_ref[...] = z[:, 1536:1792]
    gv_ref[...] = z[:, 1792:2048]
    gg_ref[...] = z[:, 2048:2304]
    ga_ref[...] = _dot(hb, wa_ref[...])
    gates_ref[...] = _dot(hb, wg_ref[...])

    cos, sin = cos_ref[...], sin_ref[...]
    lane = lax.broadcasted_iota(jnp.int32, (TILE, 256), 1)
    first_half = (lane % 16) < 8
    grp = grp_ref[...]

    def qk_norm_rope(t, g):
        tn = t * lax.rsqrt(_group_mean(t * t, grp) + EPS) * g
        partner = jnp.where(first_half, pltpu.roll(tn, 256 - 8, 1), pltpu.roll(tn, 8, 1))
        return tn * cos + partner * sin

    def max_sq_norm(tb):
        t = tb.astype(F32)
        return jnp.max(_dot((t * t).astype(BF16), grp) * DA_HEAD, axis=0, keepdims=True)

    @pl.when(pl.program_id(0) == 0)
    def _():
        qmax_ref[...] = jnp.zeros_like(qmax_ref)
        kmax_ref[...] = jnp.zeros_like(kmax_ref)

    qb = (qk_norm_rope(z[:, 768:1024], qg_ref[...]) * Q_SCALE).astype(BF16)
    kb = qk_norm_rope(z[:, 1024:1280], kg_ref[...]).astype(BF16)
    qmax_ref[...] = jnp.maximum(qmax_ref[...], max_sq_norm(qb))
    kmax_ref[...] = jnp.maximum(kmax_ref[...], max_sq_norm(kb))
    qt_ref[...] = qb.T
    for p in range(N_PAIR):
        k_ref[p] = kb[:, p * DA_HEAD:(p + 1) * DA_HEAD]
    lane_v = lax.broadcasted_iota(jnp.int32, (TILE, 128), 1)
    tail = (lane_v == DA_VDIM).astype(F32)
    for h in range(DA_HEADS):
        vh = z[:, 1280 + 128 * (h // 2):1280 + 128 * (h // 2 + 1)]
        if h % 2:
            vh = pltpu.roll(vh, DA_VDIM, 1)
        vt_ref[h] = jnp.where(lane_v < DA_VDIM, vh, tail).T[:ATT_VROWS].astype(BF16)


def _inproj(xa, mods, norm_g, w_main, w_a, w_gates, cos_t, sin_t, qg, kg, grp32):
    t_all = xa.shape[0]
    nt = t_all // TILE
    row = lambda w: pl.BlockSpec((TILE, w), lambda i: (i, 0))
    f = lambda w: jax.ShapeDtypeStruct((t_all, w), F32)
    return pl.pallas_call(
        _inproj_kernel,
        grid=(nt,),
        in_specs=[row(D_MODEL),
                  pl.BlockSpec((1, SUBLANES, D_MODEL), lambda i: (jnp.minimum(i, 1), 0, 0)),
                  _const_spec((1, D_MODEL)),
                  _const_spec((D_MODEL, W_MAIN)), _const_spec((D_MODEL, 128)), _const_spec((D_MODEL, W_GATES)),
                  row(256), row(256), _const_spec((1, 256)), _const_spec((1, 256)), _const_spec((256, 256))],
        out_specs=[row(256), row(256), row(256),
                   pl.BlockSpec((256, TILE), lambda i: (0, i)),
                   pl.BlockSpec((N_PAIR, TILE, DA_HEAD), lambda i: (0, i, 0)),
                   pl.BlockSpec((DA_HEADS, ATT_VROWS, TILE), lambda i: (0, 0, i)),
                   row(256), row(256), row(256), row(128), row(W_GATES),
                   _const_spec((SUBLANES, 256)), _const_spec((SUBLANES, 256))],
        out_shape=[f(256), f(256), f(256), jax.ShapeDtypeStruct((256, t_all), BF16),
                   jax.ShapeDtypeStruct((N_PAIR, t_all, DA_HEAD), BF16),
                   jax.ShapeDtypeStruct((DA_HEADS, ATT_VROWS, t_all), BF16),
                   f(256), f(256), f(256), f(128), f(W_GATES),
                   jax.ShapeDtypeStruct((SUBLANES, 256), F32), jax.ShapeDtypeStruct((SUBLANES, 256), F32)],
        compiler_params=_params("arbitrary"),
        name="inproj",
    )(xa, mods, norm_g, w_main, w_a, w_gates, cos_t, sin_t, qg, kg, grp32)


def _scan_tile_index(j, nt, rev):
    return jnp.where(j == 0, 0, nt - j) if rev else j


DIRECTIONS = (False, True)


def _s5_kernel(uf_ref, ur_ref, b_ref, c_ref, tab_ref, yf_ref, yr_ref, hre, him, car):
    j = pl.program_id(0)

    @pl.when(j == 0)
    def _():
        car[...] = jnp.zeros_like(car)

    for d, u_ref in enumerate((uf_ref, ur_ref)):
        bu = _dot(u_ref[...].astype(BF16), b_ref[d])
        hre[d] = bu[:, :N_STATE]
        him[d] = bu[:, N_STATE:]
    nblk = TILE // SUBLANES

    def body(b, carry):
        out = []
        for d, rev in enumerate(DIRECTIONS):
            cr, ci = carry[d]
            blk = nblk - 1 - b if rev else b
            r0 = pl.multiple_of(blk * SUBLANES, SUBLANES)
            xr = hre[d, pl.ds(r0, SUBLANES), :]
            xi = him[d, pl.ds(r0, SUBLANES), :]
            for k, s in enumerate((1, 2, 4)):
                shift = SUBLANES - s if rev else s
                sr = pltpu.roll(xr, shift, 0)
                si = pltpu.roll(xi, shift, 0)
                mr, mi = tab_ref[d, 2 * k], tab_ref[d, 2 * k + 1]
                xr, xi = xr + mr * sr - mi * si, xi + mr * si + mi * sr
            pr, pi_ = tab_ref[d, 6], tab_ref[d, 7]
            xr, xi = xr + pr * cr - pi_ * ci, xi + pr * ci + pi_ * cr
            hre[d, pl.ds(r0, SUBLANES), :] = xr
            him[d, pl.ds(r0, SUBLANES), :] = xi
            last = 0 if rev else SUBLANES - 1
            out.append((jnp.broadcast_to(xr[last:last + 1], xr.shape), jnp.broadcast_to(xi[last:last + 1], xi.shape)))
        return tuple(out)

    carry = lax.fori_loop(0, nblk, body, tuple((car[d, 0], car[d, 1]) for d in range(2)))
    for d, y_ref in enumerate((yf_ref, yr_ref)):
        car[d, 0], car[d, 1] = carry[d]
        y_ref[...] = (_dot(hre[d].astype(BF16), c_ref[d, :N_STATE]) + _dot(him[d].astype(BF16), c_ref[d, N_STATE:]))


def _dir_specs(shape, nt, index=lambda ti: (ti, 0)):
    return [pl.BlockSpec(shape, functools.partial(lambda j, rev: index(_scan_tile_index(j, nt, rev)), rev=rev))
            for rev in DIRECTIONS]


def _s5_scan(u, bcat, ccat, tabs):
    t_all = u.shape[0]
    nt = t_all // TILE
    return pl.pallas_call(
        _s5_kernel,
        grid=(nt,),
        in_specs=_dir_specs((TILE, W_SSM), nt) + [_const_spec((2, W_SSM, 2 * N_STATE)),
                                                   _const_spec((2, 2 * N_STATE, W_SSM)),
                                                   _const_spec((2, 8, SUBLANES, N_STATE))],
        out_specs=_dir_specs((TILE, W_SSM), nt),
        out_shape=[jax.ShapeDtypeStruct((t_all, W_SSM), F32)] * 2,
        scratch_shapes=[pltpu.VMEM((2, TILE, N_STATE), F32), pltpu.VMEM((2, TILE, N_STATE), F32),
                        pltpu.VMEM((2, 2, SUBLANES, N_STATE), F32)],
        compiler_params=_params("arbitrary"),
        name="s5_scan",
    )(u, u, bcat, ccat, tabs)


def _s5_tables(lam_re, lam_im, log_step, b_re, b_im, c_re, c_im, rev):
    dt = jnp.exp(log_step.astype(F32))[:, None]
    lre, lim = lam_re.astype(F32) * dt, lam_im.astype(F32) * dt
    mag = jnp.exp(lre)
    lb_re, lb_im = mag * jnp.cos(lim), mag * jnp.sin(lim)
    num_re, num_im = lb_re - 1.0, lb_im
    den = lam_re * lam_re + lam_im * lam_im
    k_re = (num_re * lam_re + num_im * lam_im) / den
    k_im = (num_im * lam_re - num_re * lam_im) / den
    bb_re = k_re[..., None] * b_re - k_im[..., None] * b_im
    bb_im = k_re[..., None] * b_im + k_im[..., None] * b_re
    eye = jnp.eye(SSM_GROUPS, dtype=F32)
    blk_in = lambda t: jnp.einsum('gpc,gh->gchp', t, eye).reshape(W_SSM, N_STATE)
    blk_out = lambda t: jnp.einsum('gcp,gh->gphc', t, eye).reshape(N_STATE, W_SSM)
    bcat = jnp.concatenate([blk_in(bb_re), blk_in(bb_im)], axis=1).astype(BF16)
    ccat = jnp.concatenate([blk_out(c_re.astype(F32)), -blk_out(c_im.astype(F32))], axis=0).astype(BF16)

    def power(k):
        k = k[:, None].astype(F32)
        m = jnp.exp(k * lre.reshape(1, -1))
        return m * jnp.cos(k * lim.reshape(1, -1)), m * jnp.sin(k * lim.reshape(1, -1))

    r = jnp.arange(SUBLANES)
    tabs = []
    for s in (1, 2, 4):
        pr, pi_ = power(jnp.full((SUBLANES,), s))
        keep = ((r + s <= SUBLANES - 1) if rev else (r >= s))[:, None]
        tabs += [jnp.where(keep, pr, 0.0), jnp.where(keep, pi_, 0.0)]
    tabs += list(power((SUBLANES - r) if rev else (r + 1)))
    return bcat, ccat, jnp.stack(tabs)


def _lru_kernel(xf_ref, xr_ref, xpf_ref, xpr_ref, xnf_ref, xnr_ref, cw_ref, cb_ref, wr_ref, br_ref, wi_ref, bi_ref,
                lam_ref, hf_ref, hr_ref, ext, a_sc, b_sc, car, *, nt):
    j = pl.program_id(0)

    @pl.when(j == 0)
    def _():
        car[...] = jnp.zeros_like(car)

    for d, (rev, x_ref, xp_ref, xn_ref) in enumerate(zip(DIRECTIONS, (xf_ref, xr_ref), (xpf_ref, xpr_ref),
                                                         (xnf_ref, xnr_ref))):
        ti = _scan_tile_index(j, nt, rev)
        prev_ok = (ti >= 2).astype(F32)
        next_ok = jnp.logical_and(ti != 0, ti != nt - 1).astype(F32)
        ext[d, 0:HALO] = xp_ref[...] * prev_ok
        ext[d, HALO:HALO + TILE] = x_ref[...]
        ext[d, HALO + TILE:] = xn_ref[...] * next_ok
        xc = cb_ref[...] + sum(cw_ref[k:k + 1] * ext[d, pl.ds(HALO - 2 + k, TILE), :] for k in range(4))
        xb = xc.astype(BF16)
        r = jax.nn.sigmoid(_dot(xb, wr_ref[d]) + br_ref[d])
        i = jax.nn.sigmoid(_dot(xb, wi_ref[d]) + bi_ref[d])
        log_a = LRU_C * r * jax.nn.log_sigmoid(lam_ref[d])
        a = jnp.exp(log_a)
        a_sc[d] = a
        b_sc[d] = jnp.sqrt(-jnp.tanh(log_a) * (a * a + 1.0)) * (i * xc)
    nblk = TILE // SUBLANES
    rows = lax.broadcasted_iota(jnp.int32, (SUBLANES, W_LRU), 0)

    def body(b, carry):
        out = []
        for d, rev in enumerate(DIRECTIONS):
            blk = nblk - 1 - b if rev else b
            r0 = pl.multiple_of(blk * SUBLANES, SUBLANES)
            a = a_sc[d, pl.ds(r0, SUBLANES), :]
            x = b_sc[d, pl.ds(r0, SUBLANES), :]
            for s in (1, 2, 4):
                shift = SUBLANES - s if rev else s
                keep = (rows + s <= SUBLANES - 1) if rev else (rows >= s)
                a_s = jnp.where(keep, pltpu.roll(a, shift, 0), 1.0)
                x_s = jnp.where(keep, pltpu.roll(x, shift, 0), 0.0)
                a, x = a * a_s, a * x_s + x
            h = x + a * carry[d]
            b_sc[d, pl.ds(r0, SUBLANES), :] = h
            last = 0 if rev else SUBLANES - 1
            out.append(jnp.broadcast_to(h[last:last + 1], h.shape))
        return tuple(out)

    carry = lax.fori_loop(0, nblk, body, (car[0], car[1]))
    for d, h_ref in enumerate((hf_ref, hr_ref)):
        car[d] = carry[d]
        h_ref[...] = b_sc[d]


def _lru_scan(lx, conv_w, conv_b, wr, br, wi, bi, lam):
    t_all = lx.shape[0]
    nt = t_all // TILE
    per = TILE // HALO
    vec = _const_spec((2, 1, W_LRU))
    return pl.pallas_call(
        functools.partial(_lru_kernel, nt=nt),
        grid=(nt,),
        in_specs=(_dir_specs((TILE, W_LRU), nt)
                  + _dir_specs((HALO, W_LRU), nt, lambda ti: (jnp.maximum(ti * per - 1, 0), 0))
                  + _dir_specs((HALO, W_LRU), nt, lambda ti: (jnp.minimum((ti + 1) * per, nt * per - 1), 0))
                  + [_const_spec((4, W_LRU)), _const_spec((1, W_LRU)), _const_spec((2, W_LRU, W_LRU)), vec,
                     _const_spec((2, W_LRU, W_LRU)), vec, vec]),
        out_specs=_dir_specs((TILE, W_LRU), nt),
        out_shape=[jax.ShapeDtypeStruct((t_all, W_LRU), F32)] * 2,
        scratch_shapes=[pltpu.VMEM((2, TILE + 2 * HALO, W_LRU), F32), pltpu.VMEM((2, TILE, W_LRU), F32),
                        pltpu.VMEM((2, TILE, W_LRU), F32), pltpu.VMEM((2, SUBLANES, W_LRU), F32)],
        compiler_params=_params("arbitrary"),
        name="lru_scan",
    )(lx, lx, lx, lx, lx, lx, conv_w, conv_b, wr, br, wi, bi, lam)


def _block_diag(w):
    n, a, b = w.shape
    return jnp.einsum('nab,nm->namb', w, jnp.eye(n, dtype=w.dtype)).reshape(n * a, n * b)


def _attn_kernel(sc_ref, qt_ref, k_ref, vt_ref, o_ref, m_sc, acc_sc, *, nkv, online):
    j = pl.program_id(1)

    @pl.when(j == 0)
    def _():
        acc_sc[...] = jnp.zeros_like(acc_sc)
        for p in range(N_PAIR):
            if online:
                m_sc[p] = jnp.full(m_sc.shape[1:], -jnp.inf, F32)
            else:
                qf = qt_ref[p * DA_HEAD:(p + 1) * DA_HEAD, :].astype(F32)
                m_sc[p] = jnp.sqrt(jnp.sum(qf * qf, axis=0, keepdims=True)) * sc_ref[0, 1 + p]

    for p in range(N_PAIR):
        s = _dot(k_ref[p], qt_ref[p * DA_HEAD:(p + 1) * DA_HEAD, :])
        if online:
            m_old = m_sc[p]
            m_new = jnp.maximum(m_old, jnp.max(s, axis=0, keepdims=True))
            m_sc[p] = m_new
            acc_sc[p] = jnp.exp2(m_old - m_new) * acc_sc[p] + _dot(vt_ref[p // 2], jnp.exp2(s - m_new).astype(BF16))
        else:
            acc_sc[p] += _dot(vt_ref[p // 2], jnp.exp2(s - m_sc[p]).astype(BF16))

    @pl.when(j == nkv - 1)
    def _():
        lam = sc_ref[0, 0]
        heads = []
        for h in range(DA_HEADS):
            a0, a1 = acc_sc[2 * h], acc_sc[2 * h + 1]
            heads.append(a0[:DA_VDIM] / a0[DA_VDIM:DA_VDIM + 1] - lam * (a1[:DA_VDIM] / a1[DA_VDIM:DA_VDIM + 1]))
        o_ref[...] = jnp.concatenate(heads, axis=0).T


def _attention(scal, qt, k, vt, tq, tk, online):
    n_q = qt.shape[1]
    nq, nkv = n_q // tq, k.shape[1] // tk
    return pl.pallas_call(
        functools.partial(_attn_kernel, nkv=nkv, online=online),
        grid=(nq, nkv),
        in_specs=[pl.BlockSpec(memory_space=pltpu.SMEM),
                  pl.BlockSpec((256, tq), lambda i, j: (0, i)),
                  pl.BlockSpec((N_PAIR, tk, DA_HEAD), lambda i, j: (0, j, 0)),
                  pl.BlockSpec((DA_HEADS, ATT_VROWS, tk), lambda i, j: (0, 0, j))],
        out_specs=pl.BlockSpec((tq, DA_HEADS * DA_VDIM), lambda i, j: (i, 0)),
        out_shape=jax.ShapeDtypeStruct((n_q, DA_HEADS * DA_VDIM), F32),
        scratch_shapes=[pltpu.VMEM((N_PAIR, 1, tq), F32), pltpu.VMEM((N_PAIR, ATT_VROWS, tq), F32)],
        compiler_params=_params("arbitrary", "arbitrary"),
        name="diff_attn_online" if online else "diff_attn",
    )(scal, qt, k, vt)


def _gla_kernel(qkf_ref, qkr_ref, vf_ref, vr_ref, gaf_ref, gar_ref, wa_ref, ba_ref, tri_ref, sel_ref, bd_ref,
                of_ref, or_ref, s_sc):
    j = pl.program_id(0)

    @pl.when(j == 0)
    def _():
        s_sc[...] = jnp.zeros_like(s_sc)

    bd = bd_ref[...]
    ii = lax.broadcasted_iota(jnp.int32, (GLA_CHUNK, GLA_CHUNK), 0)
    jj = lax.broadcasted_iota(jnp.int32, (GLA_CHUNK, GLA_CHUNK), 1)
    lane_k = lax.broadcasted_iota(jnp.int32, (GLA_CHUNK, W_GLA_K), 1) // GLA_DK
    lane_v = lax.broadcasted_iota(jnp.int32, (GLA_CHUNK, 256), 1) // GLA_DV
    nchunk = TILE // GLA_CHUNK
    states = [s_sc[0], s_sc[1]]
    for d, (rev, qk_ref, v_ref, ga_ref, o_ref) in enumerate(zip(DIRECTIONS, (qkf_ref, qkr_ref), (vf_ref, vr_ref),
                                                                (gaf_ref, gar_ref), (of_ref, or_ref))):
        q = qk_ref[:, :W_GLA_K] * (GLA_DK ** -0.5)
        k = qk_ref[:, W_GLA_K:]
        v = v_ref[...]
        lg = jax.nn.log_sigmoid(_dot(ga_ref[...], wa_ref[d], HIGHEST) + ba_ref[d]) / GLA_TAU
        lg_hi = lg.astype(BF16)
        lg_lo = (lg - lg_hi.astype(F32)).astype(BF16)
        b = _dot(tri_ref[d], lg_hi) + _dot(tri_ref[d], lg_lo)
        b_last = _dot(sel_ref[...], lg_hi) + _dot(sel_ref[...], lg_lo)
        qc = q * jnp.exp(b)
        kinv = k * jnp.exp(-b)
        kdec_t = (k * jnp.exp(b_last - b)).T
        decay_t = jnp.exp(b_last).T
        causal = (jj >= ii) if rev else (jj <= ii)
        s_state = states[d]
        for ci in range(nchunk):
            c = nchunk - 1 - ci if rev else ci
            rs = slice(c * GLA_CHUNK, (c + 1) * GLA_CHUNK)
            qcc, kic, vc = qc[rs], kinv[rs], v[rs]
            o = _dot(qcc, s_state)
            for h in range(GLA_HEADS):
                att = _dot_nt(jnp.where(lane_k == h, qcc, 0.0), kic)
                att = jnp.where(causal, att, 0.0)
                o = o + jnp.where(lane_v == h, _dot(att, vc), 0.0)
            o_ref[rs, :] = o
            kv = _dot(kdec_t[:, rs], vc) * bd
            s_state = decay_t[:, c * GLA_CHUNK:c * GLA_CHUNK + 1] * s_state + kv
        states[d] = s_state
    s_sc[0], s_sc[1] = states


def _gla_scan(gqk, gv, ga, wa, ba, tri, sel, bd):
    t_all = gqk.shape[0]
    nt = t_all // TILE
    return pl.pallas_call(
        _gla_kernel,
        grid=(nt,),
        in_specs=(_dir_specs((TILE, 256), nt) + _dir_specs((TILE, 256), nt) + _dir_specs((TILE, 128), nt)
                  + [_const_spec((2, 128, W_GLA_K)), _const_spec((2, 1, W_GLA_K)),
                     _const_spec((2, TILE, TILE)), _const_spec((TILE, TILE)), _const_spec((W_GLA_K, 256))]),
        out_specs=_dir_specs((TILE, 256), nt),
        out_shape=[jax.ShapeDtypeStruct((t_all, 256), F32)] * 2,
        scratch_shapes=[pltpu.VMEM((2, W_GLA_K, 256), F32)],
        compiler_params=_params("arbitrary"),
        name="gla_scan",
    )(gqk, gqk, gv, gv, ga, ga, wa, ba, tri, sel, bd)


def _merge_kernel(x_ref, mod_ref, u_ref, yf_ref, yb_ref, d_ref, wglu_ref, hf_ref, hb_ref, ly_ref,
                  att_ref, dan_ref, gf_ref, gb_ref, gg_ref, gn_ref, grp_ref, gates_ref, wb_ref, wo_ref,
                  o_ref, *, da_scale):
    z = jax.nn.gelu(u_ref[...] * d_ref[...] + yf_ref[...] + yb_ref[...])
    ya = z * jax.nn.sigmoid(_dot(z.astype(BF16), wglu_ref[...]))
    yb = (hf_ref[...] + hb_ref[...]) * jax.nn.gelu(ly_ref[...])
    og = gf_ref[...] + gb_ref[...]
    grp = grp_ref[...]
    gg = gg_ref[...]
    yd = og * lax.rsqrt(_group_mean(og * og, grp) + EPS) * gn_ref[...] * (gg * jax.nn.sigmoid(gg))
    att = att_ref[...]
    yc = att * lax.rsqrt(_group_mean(att * att, grp) + EPS) * (dan_ref[...] * da_scale)

    def gate(n):
        return jax.nn.sigmoid(gates_ref[:, n * D_MODEL:(n + 1) * D_MODEL])

    acc = gate(0) * _dot(ya.astype(BF16), wb_ref[0])
    for n, y in ((1, yb), (2, yc), (3, yd)):
        acc = acc + gate(n) * _dot(y.astype(BF16), wb_ref[n])
    o_ref[...] = x_ref[...] + mod_ref[0, 2:3] * _dot(acc.astype(BF16), wo_ref[...])


def _merge(xa, mods, u, yf, yb, ssm_d, w_glu, hf, hb, ly, att, da_norm, gf, gb, gg, gla_norm, grp64, gates,
           w_branch, w_out, da_scale):
    t_all = xa.shape[0]
    nt = t_all // TILE
    row = lambda w: pl.BlockSpec((TILE, w), lambda i: (i, 0))
    vec = lambda w: _const_spec((1, w))
    return pl.pallas_call(
        functools.partial(_merge_kernel, da_scale=da_scale),
        grid=(nt,),
        in_specs=[row(D_MODEL), pl.BlockSpec((1, SUBLANES, D_MODEL), lambda i: (jnp.minimum(i, 1), 0, 0)),
                  row(256), row(256), row(256), vec(256), _const_spec((256, 256)),
                  row(256), row(256), row(256),
                  row(256), vec(256),
                  row(256), row(256), row(256), vec(256), _const_spec((256, 256)), row(W_GATES),
                  _const_spec((N_BRANCH, W_BRANCH, D_MODEL)), _const_spec((D_MODEL, D_MODEL))],
        out_specs=row(D_MODEL),
        out_shape=jax.ShapeDtypeStruct((t_all, D_MODEL), F32),
        compiler_params=_params("arbitrary"),
        name="merge",
    )(xa, mods, u, yf, yb, ssm_d, w_glu, hf, hb, ly, att, da_norm, gf, gb, gg, gla_norm, grp64, gates,
      w_branch, w_out)


def _ffn_kernel(x_ref, xp_ref, xn_ref, mod_ref, g_ref, wa_ref, wg_ref, cw_ref, cb_ref, wd_ref, o_ref, ext, gt_sc,
                *, nt):
    i = pl.program_id(0)
    mod = mod_ref[0]
    g = g_ref[...]
    norm = lambda t: _rms(t, g) * (1.0 + mod[4:5]) + mod[3:4]
    prev_ok = (i >= 2).astype(F32)
    next_ok = jnp.logical_and(i != 0, i != nt - 1).astype(F32)
    x = x_ref[...]
    hn = norm(x)
    ext[0:HALO] = norm(xp_ref[...]) * prev_ok
    ext[HALO:HALO + TILE] = hn
    ext[HALO + TILE:] = norm(xn_ref[...]) * next_ok
    gt_sc[...] = _dot(ext[...].astype(BF16), wg_ref[...])
    gt = cb_ref[...] + sum(cw_ref[k:k + 1] * gt_sc[pl.ds(HALO - 1 + k, TILE), :] for k in range(3))
    a = _dot(hn.astype(BF16), wa_ref[...])
    y = _dot((jax.nn.gelu(gt) * a).astype(BF16), wd_ref[...])
    o_ref[...] = x + mod[5:6] * y


def _ffn(xa, mods, norm_g, w_up_a, w_up_g, conv_w, conv_b, w_down):
    t_all = xa.shape[0]
    nt = t_all // TILE
    per = TILE // HALO
    return pl.pallas_call(
        functools.partial(_ffn_kernel, nt=nt),
        grid=(nt,),
        in_specs=[pl.BlockSpec((TILE, D_MODEL), lambda i: (i, 0)),
                  pl.BlockSpec((HALO, D_MODEL), lambda i: (jnp.maximum(i * per - 1, 0), 0)),
                  pl.BlockSpec((HALO, D_MODEL), lambda i: (jnp.minimum((i + 1) * per, nt * per - 1), 0)),
                  pl.BlockSpec((1, SUBLANES, D_MODEL), lambda i: (jnp.minimum(i, 1), 0, 0)),
                  _const_spec((1, D_MODEL)), _const_spec((D_MODEL, D_FF)), _const_spec((D_MODEL, D_FF)),
                  _const_spec((3, D_FF)), _const_spec((1, D_FF)), _const_spec((D_FF, D_MODEL))],
        out_specs=pl.BlockSpec((TILE, D_MODEL), lambda i: (i, 0)),
        out_shape=jax.ShapeDtypeStruct((t_all, D_MODEL), F32),
        scratch_shapes=[pltpu.VMEM((TILE + 2 * HALO, D_MODEL), F32), pltpu.VMEM((TILE + 2 * HALO, D_FF), F32)],
        compiler_params=_params("arbitrary"),
        name="convffn",
    )(xa, xa, xa, mods, norm_g, w_up_a, w_up_g, conv_w, conv_b, w_down)


def _rope_tables(n_lat, n_ctx):
    pos = jnp.arange(n_lat)
    row, col = (pos // GRID_W).astype(F32), (pos % GRID_W).astype(F32)
    half = DA_HEAD // 2
    inv = 1.0 / (ROPE_BASE ** (jnp.arange(0, half, 2, dtype=F32) / half))
    ang = jnp.stack([row[:, None] * inv, col[:, None] * inv], axis=1)
    cos, sin = jnp.cos(ang), jnp.sin(ang)
    cos32 = jnp.stack([cos, cos], axis=2).reshape(n_lat, DA_HEAD)
    sin32 = jnp.stack([-sin, sin], axis=2).reshape(n_lat, DA_HEAD)
    cos_t = jnp.concatenate([jnp.ones((n_ctx, DA_HEAD), F32), cos32], axis=0)
    sin_t = jnp.concatenate([jnp.zeros((n_ctx, DA_HEAD), F32), sin32], axis=0)
    return jnp.tile(cos_t, (1, N_PAIR)), jnp.tile(sin_t, (1, N_PAIR))


def _group_mean_matrix(width, group):
    idx = jnp.arange(width) // group
    return ((idx[:, None] == idx[None, :]).astype(F32) / group).astype(BF16)


def _gla_chunk_matrices(rev):
    t = jnp.arange(TILE)
    same = (t[:, None] // GLA_CHUNK) == (t[None, :] // GLA_CHUNK)
    order = (t[None, :] >= t[:, None]) if rev else (t[None, :] <= t[:, None])
    return (same & order).astype(BF16), same.astype(BF16)


def kernel(x, c, ctx, c_ctx, w_ada, b_ada, norm1_g, norm2_g, w_in, ssm_lam_re, ssm_lam_im, ssm_log_step, ssm_b_re, ssm_b_im, ssm_c_re, ssm_c_im, ssm_d, ssm_w_glu, lru_conv_w, lru_conv_b, lru_wr, lru_br, lru_wi, lru_bi, lru_lam, da_q_norm, da_k_norm, da_lam, da_out_norm, gla_wa2, gla_ba, gla_out_norm, w_branch, w_out, w_up, ffn_conv_w, ffn_conv_b, w_down):
    depth = w_in.shape[0]
    n_lat, n_ctx = x.shape[1], ctx.shape[1]
    assert x.shape[0] == 1 and n_ctx == TILE and n_lat % TILE == 0
    xa = jnp.concatenate([ctx[0], x[0]], axis=0).astype(F32)

    cond = jnp.zeros((SUBLANES, D_MODEL), F32).at[0].set(c_ctx).at[1].set(c[0])
    ada = _adaln(cond, w_ada, b_ada)
    mods_all = jnp.pad(ada[:, :2].reshape(depth, 2, 6, D_MODEL), ((0, 0), (0, 0), (0, SUBLANES - 6), (0, 0)))

    cos_t, sin_t = _rope_tables(n_lat, n_ctx)
    grp32 = _group_mean_matrix(256, DA_HEAD)
    grp64 = _group_mean_matrix(256, GLA_DV)
    bd_mask = ((jnp.arange(W_GLA_K)[:, None] // GLA_DK) == (jnp.arange(256)[None, :] // GLA_DV)).astype(F32)
    gla_tri = jnp.stack([_gla_chunk_matrices(rev)[0] for rev in DIRECTIONS])
    gla_sel = _gla_chunk_matrices(False)[1]
    tk = TILE * max(d for d in range(1, ATT_TK_TILES + 1) if ((n_ctx + n_lat) // TILE) % d == 0)
    tq = min(ATT_TQ, n_lat)

    for l in range(depth):
        mods = mods_all[l]
        w_l = w_in[l]
        w_main = w_l[:, :W_MAIN].astype(BF16)
        w_a = jnp.pad(w_l[:, W_MAIN:W_MAIN + 2 * GLA_RANK], ((0, 0), (0, 128 - 2 * GLA_RANK))).astype(BF16)
        w_gates = w_l[:, W_MAIN + 2 * GLA_RANK:].astype(BF16)
        qg = jnp.tile(da_q_norm[l], N_PAIR)[None]
        kg = jnp.tile(da_k_norm[l], N_PAIR)[None]
        u, lx, ly, qt, k, vt, gqk, gv, gg, ga, gates, qmax2, kmax2 = _inproj(
            xa, mods, norm1_g[l][None], w_main, w_a, w_gates, cos_t, sin_t, qg, kg, grp32)

        s5_tabs = [_s5_tables(ssm_lam_re[l, di], ssm_lam_im[l, di], ssm_log_step[l, di], ssm_b_re[l, di],
                              ssm_b_im[l, di], ssm_c_re[l, di], ssm_c_im[l, di], rev)
                   for di, rev in enumerate(DIRECTIONS)]
        ys = _s5_scan(u, *(jnp.stack(t) for t in zip(*s5_tabs)))
        hs = _lru_scan(lx, lru_conv_w[l], lru_conv_b[l][None],
                       jnp.stack([_block_diag(lru_wr[l, di]) for di in range(2)]).astype(BF16), lru_br[l][:, None],
                       jnp.stack([_block_diag(lru_wi[l, di]) for di in range(2)]).astype(BF16), lru_bi[l][:, None],
                       lru_lam[l][:, None])
        wa = jnp.stack([jnp.zeros((128, W_GLA_K), F32).at[di * GLA_RANK:(di + 1) * GLA_RANK].set(gla_wa2[l, di])
                        for di in range(2)])
        gs = _gla_scan(gqk, gv, ga, wa, gla_ba[l][:, None], gla_tri, gla_sel, bd_mask)

        lp = da_lam[l].astype(F32)
        lam_init = 0.8 - 0.6 * math.exp(-0.3 * l)
        lam = jnp.exp(jnp.sum(lp[0] * lp[1])) - jnp.exp(jnp.sum(lp[2] * lp[3])) + lam_init
        qmax = jnp.sqrt(qmax2[0, ::DA_HEAD])
        kmax = jnp.sqrt(kmax2[0, ::DA_HEAD])
        scal = jnp.concatenate([lam.reshape(1), kmax, jnp.zeros((7,), F32)]).reshape(1, 16)

        def attend(online):
            att_lat = _attention(scal, qt[:, n_ctx:], k, vt, tq, tk, online)
            att_ctx = _attention(scal, qt[:, :n_ctx], k[:, :n_ctx], vt[:, :, :n_ctx], n_ctx, n_ctx, online)
            return att_ctx, att_lat

        safe = 2.0 * jnp.max(qmax * kmax) < MAX_SHIFT_SPREAD
        att_ctx, att_lat = lax.cond(safe, lambda: attend(False), lambda: attend(True))
        att = jnp.concatenate([att_ctx, att_lat], axis=0)

        xa = _merge(xa, mods, u, ys[0], ys[1], ssm_d[l][None], ssm_w_glu[l].astype(BF16), hs[0], hs[1], ly,
                    att, jnp.tile(da_out_norm[l], DA_HEADS)[None], gs[0], gs[1], gg,
                    jnp.tile(gla_out_norm[l], GLA_HEADS)[None],
                    grp64, gates, w_branch[l].astype(BF16), w_out[l].astype(BF16), 1.0 - lam_init)
        xa = _ffn(xa, mods, norm2_g[l][None], w_up[l][:, :D_FF].astype(BF16), w_up[l][:, D_FF:].astype(BF16),
                  ffn_conv_w[l], ffn_conv_b[l][None], w_down[l].astype(BF16))
    return xa[n_ctx:][None].astype(x.dtype)
```

```python
import functools
import math

import jax
import jax.numpy as jnp
from jax import lax
from jax.experimental import pallas as pl
from jax.experimental.pallas import tpu as pltpu

F32 = jnp.float32
BF16 = jnp.bfloat16
HIGHEST = lax.Precision.HIGHEST

D_MODEL = 1024
EPS = 1e-6
GRID_W = 64
TILE = 256
SUBLANES = 8
HALO = SUBLANES
VMEM_LIMIT = 56 * 1024 * 1024

SSM_GROUPS, SSM_GROUP, SSM_STATE = 16, 16, 64
W_SSM = 256
N_STATE = SSM_GROUPS * SSM_STATE
W_LRU, LRU_BLOCKS, LRU_BLOCK, LRU_C = 256, 8, 32, 8.0
DA_HEADS, DA_HEAD, DA_VDIM = 4, 32, 64
N_PAIR = 2 * DA_HEADS
ROPE_BASE = 10000.0
GLA_HEADS, GLA_DK, GLA_DV, GLA_RANK, GLA_TAU, GLA_CHUNK = 4, 32, 64, 16, 16.0, 64
W_GLA_K = GLA_HEADS * GLA_DK
N_BRANCH, W_BRANCH, D_FF = 4, 256, 2816
W_MAIN = 2304
QKV_START, W_QKV = 768, 768
W_GATES = N_BRANCH * D_MODEL
Q_SCALE = DA_HEAD ** -0.5 * math.log2(math.e)
MAX_SHIFT_SPREAD = 100.0
ATT_TQ = 512
ATT_TK_TILES = 13
ATT_VROWS = 80


def _dot(a, b, precision=None):
    return jnp.dot(a, b, preferred_element_type=F32, precision=precision)


def _dot_nt(a, b, precision=None):
    return lax.dot_general(a, b, (((1,), (1,)), ((), ())), preferred_element_type=F32, precision=precision)


def _params(*sem):
    return pltpu.CompilerParams(dimension_semantics=sem, vmem_limit_bytes=VMEM_LIMIT)


def _const_spec(shape):
    nd = len(shape)
    return pl.BlockSpec(shape, lambda *_: (0,) * nd)


def _rms(x, g):
    return x * lax.rsqrt(jnp.mean(x * x, axis=-1, keepdims=True) + EPS) * g


def _group_mean(sq, grp):
    hi = sq.astype(BF16)
    lo = (sq - hi.astype(F32)).astype(BF16)
    return _dot(hi, grp) + _dot(lo, grp)


def _adaln_kernel(c_ref, w_ref, b_ref, o_ref):
    c = c_ref[...]
    o_ref[0] = _dot(c * jax.nn.sigmoid(c), w_ref[0], HIGHEST) + b_ref[0]


def _adaln(cond, w_ada, b_ada):
    depth, _, n_out = w_ada.shape
    blk = 1536
    return pl.pallas_call(
        _adaln_kernel,
        grid=(depth, n_out // blk),
        in_specs=[pl.BlockSpec((SUBLANES, D_MODEL), lambda l, n: (0, 0)),
                  pl.BlockSpec((1, D_MODEL, blk), lambda l, n: (l, 0, n)),
                  pl.BlockSpec((1, 1, blk), lambda l, n: (l, 0, n))],
        out_specs=pl.BlockSpec((1, SUBLANES, blk), lambda l, n: (l, 0, n)),
        out_shape=jax.ShapeDtypeStruct((depth, SUBLANES, n_out), F32),
        compiler_params=_params("arbitrary", "arbitrary"),
        name="adaln",
    )(cond, w_ada, b_ada.reshape(depth, 1, n_out))


def _inproj_kernel(x_ref, mod_ref, g_ref, wqkv_ref, wm_ref, wa_ref, wg_ref, cos_ref, sin_ref, qg_ref, kg_ref, grp_ref,
                   u_ref, lx_ref, ly_ref, qt_ref, k_ref, vt_ref, gqk_ref, gv_ref, gg_ref, ga_ref, gates_ref,
                   qmax_ref, kmax_ref):
    x = x_ref[...]
    mod = mod_ref[0]
    hn = _rms(x, g_ref[...]) * (1.0 + mod[1:2]) + mod[0:1]
    hb = hn.astype(BF16)
    zq = _dot(hb, wqkv_ref[...])

    cos, sin = cos_ref[...], sin_ref[...]
    lane = lax.broadcasted_iota(jnp.int32, (TILE, 256), 1)
    first_half = (lane % 16) < 8
    grp = grp_ref[...]

    def qk_norm_rope(t, g):
        tn = t * lax.rsqrt(_group_mean(t * t, grp) + EPS) * g
        partner = jnp.where(first_half, pltpu.roll(tn, 256 - 8, 1), pltpu.roll(tn, 8, 1))
        return tn * cos + partner * sin

    def max_sq_norm(tb):
        t = tb.astype(F32)
        return jnp.max(_dot((t * t).astype(BF16), grp) * DA_HEAD, axis=0, keepdims=True)

    @pl.when(pl.program_id(0) == 0)
    def _():
        qmax_ref[...] = jnp.zeros_like(qmax_ref)
        kmax_ref[...] = jnp.zeros_like(kmax_ref)

    qb = (qk_norm_rope(zq[:, 0:256], qg_ref[...]) * Q_SCALE).astype(BF16)
    kb = qk_norm_rope(zq[:, 256:512], kg_ref[...]).astype(BF16)
    qmax_ref[...] = jnp.maximum(qmax_ref[...], max_sq_norm(qb))
    kmax_ref[...] = jnp.maximum(kmax_ref[...], max_sq_norm(kb))
    qt_ref[...] = qb.T
    for p in range(N_PAIR):
        k_ref[p] = kb[:, p * DA_HEAD:(p + 1) * DA_HEAD]
    lane_v = lax.broadcasted_iota(jnp.int32, (TILE, 128), 1)
    tail = (lane_v == DA_VDIM).astype(F32)
    for h in range(DA_HEADS):
        vh = zq[:, 512 + 128 * (h // 2):512 + 128 * (h // 2 + 1)]
        if h % 2:
            vh = pltpu.roll(vh, DA_VDIM, 1)
        vt_ref[h] = jnp.where(lane_v < DA_VDIM, vh, tail).T[:ATT_VROWS].astype(BF16)

    z = _dot(hb, wm_ref[...])
    u_ref[...] = z[:, 0:256]
    lx_ref[...] = z[:, 256:512]
    ly_ref[...] = z[:, 512:768]
    gqk_ref[...] = z[:, 768:1024]
    gv_ref[...] = z[:, 1024:1280]
    gg_ref[...] = z[:, 1280:1536]
    ga_ref[...] = _dot(hb, wa_ref[...])
    gates_ref[...] = _dot(hb, wg_ref[...])


def _inproj(xa, mods, norm_g, w_qkv, w_main, w_a, w_gates, cos_t, sin_t, qg, kg, grp32):
    t_all = xa.shape[0]
    nt = t_all // TILE
    row = lambda w: pl.BlockSpec((TILE, w), lambda i: (i, 0))
    f = lambda w: jax.ShapeDtypeStruct((t_all, w), F32)
    return pl.pallas_call(
        _inproj_kernel,
        grid=(nt,),
        in_specs=[row(D_MODEL),
                  pl.BlockSpec((1, SUBLANES, D_MODEL), lambda i: (jnp.minimum(i, 1), 0, 0)),
                  _const_spec((1, D_MODEL)),
                  _const_spec((D_MODEL, W_QKV)), _const_spec((D_MODEL, W_MAIN - W_QKV)), _const_spec((D_MODEL, 128)),
                  _const_spec((D_MODEL, W_GATES)),
                  row(256), row(256), _const_spec((1, 256)), _const_spec((1, 256)), _const_spec((256, 256))],
        out_specs=[row(256), row(256), row(256),
                   pl.BlockSpec((256, TILE), lambda i: (0, i)),
                   pl.BlockSpec((N_PAIR, TILE, DA_HEAD), lambda i: (0, i, 0)),
                   pl.BlockSpec((DA_HEADS, ATT_VROWS, TILE), lambda i: (0, 0, i)),
                   row(256), row(256), row(256), row(128), row(W_GATES),
                   _const_spec((SUBLANES, 256)), _const_spec((SUBLANES, 256))],
        out_shape=[f(256), f(256), f(256), jax.ShapeDtypeStruct((256, t_all), BF16),
                   jax.ShapeDtypeStruct((N_PAIR, t_all, DA_HEAD), BF16),
                   jax.ShapeDtypeStruct((DA_HEADS, ATT_VROWS, t_all), BF16),
                   f(256), f(256), f(256), f(128), f(W_GATES),
                   jax.ShapeDtypeStruct((SUBLANES, 256), F32), jax.ShapeDtypeStruct((SUBLANES, 256), F32)],
        compiler_params=_params("arbitrary"),
        name="inproj",
    )(xa, mods, norm_g, w_qkv, w_main, w_a, w_gates, cos_t, sin_t, qg, kg, grp32)


def _scan_tile_index(j, nt, rev):
    return jnp.where(j == 0, 0, nt - j) if rev else j


DIRECTIONS = (False, True)


def _s5_kernel(uf_ref, ur_ref, b_ref, c_ref, tab_ref, yf_ref, yr_ref, hre, him, car):
    j = pl.program_id(0)

    @pl.when(j == 0)
    def _():
        car[...] = jnp.zeros_like(car)

    for d, u_ref in enumerate((uf_ref, ur_ref)):
        bu = _dot(u_ref[...].astype(BF16), b_ref[d])
        hre[d] = bu[:, :N_STATE]
        him[d] = bu[:, N_STATE:]
    nblk = TILE // SUBLANES

    def body(b, carry):
        out = []
        for d, rev in enumerate(DIRECTIONS):
            cr, ci = carry[d]
            blk = nblk - 1 - b if rev else b
            r0 = pl.multiple_of(blk * SUBLANES, SUBLANES)
            xr = hre[d, pl.ds(r0, SUBLANES), :]
            xi = him[d, pl.ds(r0, SUBLANES), :]
            for k, s in enumerate((1, 2, 4)):
                shift = SUBLANES - s if rev else s
                sr = pltpu.roll(xr, shift, 0)
                si = pltpu.roll(xi, shift, 0)
                mr, mi = tab_ref[d, 2 * k], tab_ref[d, 2 * k + 1]
                xr, xi = xr + mr * sr - mi * si, xi + mr * si + mi * sr
            pr, pi_ = tab_ref[d, 6], tab_ref[d, 7]
            xr, xi = xr + pr * cr - pi_ * ci, xi + pr * ci + pi_ * cr
            hre[d, pl.ds(r0, SUBLANES), :] = xr
            him[d, pl.ds(r0, SUBLANES), :] = xi
            last = 0 if rev else SUBLANES - 1
            out.append((jnp.broadcast_to(xr[last:last + 1], xr.shape), jnp.broadcast_to(xi[last:last + 1], xi.shape)))
        return tuple(out)

    carry = lax.fori_loop(0, nblk, body, tuple((car[d, 0], car[d, 1]) for d in range(2)))
    for d, y_ref in enumerate((yf_ref, yr_ref)):
        car[d, 0], car[d, 1] = carry[d]
        y_ref[...] = (_dot(hre[d].astype(BF16), c_ref[d, :N_STATE]) + _dot(him[d].astype(BF16), c_ref[d, N_STATE:]))


def _dir_specs(shape, nt, index=lambda ti: (ti, 0)):
    return [pl.BlockSpec(shape, functools.partial(lambda j, rev: index(_scan_tile_index(j, nt, rev)), rev=rev))
            for rev in DIRECTIONS]


def _s5_scan(u, bcat, ccat, tabs):
    t_all = u.shape[0]
    nt = t_all // TILE
    return pl.pallas_call(
        _s5_kernel,
        grid=(nt,),
        in_specs=_dir_specs((TILE, W_SSM), nt) + [_const_spec((2, W_SSM, 2 * N_STATE)),
                                                   _const_spec((2, 2 * N_STATE, W_SSM)),
                                                   _const_spec((2, 8, SUBLANES, N_STATE))],
        out_specs=_dir_specs((TILE, W_SSM), nt),
        out_shape=[jax.ShapeDtypeStruct((t_all, W_SSM), F32)] * 2,
        scratch_shapes=[pltpu.VMEM((2, TILE, N_STATE), F32), pltpu.VMEM((2, TILE, N_STATE), F32),
                        pltpu.VMEM((2, 2, SUBLANES, N_STATE), F32)],
        compiler_params=_params("arbitrary"),
        name="s5_scan",
    )(u, u, bcat, ccat, tabs)


def _s5_tables(lam_re, lam_im, log_step, b_re, b_im, c_re, c_im, rev):
    dt = jnp.exp(log_step.astype(F32))[:, None]
    lre, lim = lam_re.astype(F32) * dt, lam_im.astype(F32) * dt
    mag = jnp.exp(lre)
    lb_re, lb_im = mag * jnp.cos(lim), mag * jnp.sin(lim)
    num_re, num_im = lb_re - 1.0, lb_im
    den = lam_re * lam_re + lam_im * lam_im
    k_re = (num_re * lam_re + num_im * lam_im) / den
    k_im = (num_im * lam_re - num_re * lam_im) / den
    bb_re = k_re[..., None] * b_re - k_im[..., None] * b_im
    bb_im = k_re[..., None] * b_im + k_im[..., None] * b_re
    eye = jnp.eye(SSM_GROUPS, dtype=F32)
    blk_in = lambda t: jnp.einsum('gpc,gh->gchp', t, eye).reshape(W_SSM, N_STATE)
    blk_out = lambda t: jnp.einsum('gcp,gh->gphc', t, eye).reshape(N_STATE, W_SSM)
    bcat = jnp.concatenate([blk_in(bb_re), blk_in(bb_im)], axis=1).astype(BF16)
    ccat = jnp.concatenate([blk_out(c_re.astype(F32)), -blk_out(c_im.astype(F32))], axis=0).astype(BF16)

    def power(k):
        k = k[:, None].astype(F32)
        m = jnp.exp(k * lre.reshape(1, -1))
        return m * jnp.cos(k * lim.reshape(1, -1)), m * jnp.sin(k * lim.reshape(1, -1))

    r = jnp.arange(SUBLANES)
    tabs = []
    for s in (1, 2, 4):
        pr, pi_ = power(jnp.full((SUBLANES,), s))
        keep = ((r + s <= SUBLANES - 1) if rev else (r >= s))[:, None]
        tabs += [jnp.where(keep, pr, 0.0), jnp.where(keep, pi_, 0.0)]
    tabs += list(power((SUBLANES - r) if rev else (r + 1)))
    return bcat, ccat, jnp.stack(tabs)


def _lru_kernel(xf_ref, xr_ref, xpf_ref, xpr_ref, xnf_ref, xnr_ref, cw_ref, cb_ref, wr_ref, br_ref, wi_ref, bi_ref,
                lam_ref, hf_ref, hr_ref, ext, a_sc, b_sc, car, *, nt):
    j = pl.program_id(0)

    @pl.when(j == 0)
    def _():
        car[...] = jnp.zeros_like(car)

    for d, (rev, x_ref, xp_ref, xn_ref) in enumerate(zip(DIRECTIONS, (xf_ref, xr_ref), (xpf_ref, xpr_ref),
                                                         (xnf_ref, xnr_ref))):
        ti = _scan_tile_index(j, nt, rev)
        prev_ok = (ti >= 2).astype(F32)
        next_ok = jnp.logical_and(ti != 0, ti != nt - 1).astype(F32)
        ext[d, 0:HALO] = xp_ref[...] * prev_ok
        ext[d, HALO:HALO + TILE] = x_ref[...]
        ext[d, HALO + TILE:] = xn_ref[...] * next_ok
        xc = cb_ref[...] + sum(cw_ref[k:k + 1] * ext[d, pl.ds(HALO - 2 + k, TILE), :] for k in range(4))
        xb = xc.astype(BF16)
        r = jax.nn.sigmoid(_dot(xb, wr_ref[d]) + br_ref[d])
        i = jax.nn.sigmoid(_dot(xb, wi_ref[d]) + bi_ref[d])
        log_a = LRU_C * r * jax.nn.log_sigmoid(lam_ref[d])
        a = jnp.exp(log_a)
        a_sc[d] = a
        b_sc[d] = jnp.sqrt(-jnp.tanh(log_a) * (a * a + 1.0)) * (i * xc)
    nblk = TILE // SUBLANES
    rows = lax.broadcasted_iota(jnp.int32, (SUBLANES, W_LRU), 0)

    def body(b, carry):
        out = []
        for d, rev in enumerate(DIRECTIONS):
            blk = nblk - 1 - b if rev else b
            r0 = pl.multiple_of(blk * SUBLANES, SUBLANES)
            a = a_sc[d, pl.ds(r0, SUBLANES), :]
            x = b_sc[d, pl.ds(r0, SUBLANES), :]
            for s in (1, 2, 4):
                shift = SUBLANES - s if rev else s
                keep = (rows + s <= SUBLANES - 1) if rev else (rows >= s)
                a_s = jnp.where(keep, pltpu.roll(a, shift, 0), 1.0)
                x_s = jnp.where(keep, pltpu.roll(x, shift, 0), 0.0)
                a, x = a * a_s, a * x_s + x
            h = x + a * carry[d]
            b_sc[d, pl.ds(r0, SUBLANES), :] = h
            last = 0 if rev else SUBLANES - 1
            out.append(jnp.broadcast_to(h[last:last + 1], h.shape))
        return tuple(out)

    carry = lax.fori_loop(0, nblk, body, (car[0], car[1]))
    for d, h_ref in enumerate((hf_ref, hr_ref)):
        car[d] = carry[d]
        h_ref[...] = b_sc[d]


def _lru_scan(lx, conv_w, conv_b, wr, br, wi, bi, lam):
    t_all = lx.shape[0]
    nt = t_all // TILE
    per = TILE // HALO
    vec = _const_spec((2, 1, W_LRU))
    return pl.pallas_call(
        functools.partial(_lru_kernel, nt=nt),
        grid=(nt,),
        in_specs=(_dir_specs((TILE, W_LRU), nt)
                  + _dir_specs((HALO, W_LRU), nt, lambda ti: (jnp.maximum(ti * per - 1, 0), 0))
                  + _dir_specs((HALO, W_LRU), nt, lambda ti: (jnp.minimum((ti + 1) * per, nt * per - 1), 0))
                  + [_const_spec((4, W_LRU)), _const_spec((1, W_LRU)), _const_spec((2, W_LRU, W_LRU)), vec,
                     _const_spec((2, W_LRU, W_LRU)), vec, vec]),
        out_specs=_dir_specs((TILE, W_LRU), nt),
        out_shape=[jax.ShapeDtypeStruct((t_all, W_LRU), F32)] * 2,
        scratch_shapes=[pltpu.VMEM((2, TILE + 2 * HALO, W_LRU), F32), pltpu.VMEM((2, TILE, W_LRU), F32),
                        pltpu.VMEM((2, TILE, W_LRU), F32), pltpu.VMEM((2, SUBLANES, W_LRU), F32)],
        compiler_params=_params("arbitrary"),
        name="lru_scan",
    )(lx, lx, lx, lx, lx, lx, conv_w, conv_b, wr, br, wi, bi, lam)


def _block_diag(w):
    n, a, b = w.shape
    return jnp.einsum('nab,nm->namb', w, jnp.eye(n, dtype=w.dtype)).reshape(n * a, n * b)


def _attn_kernel(sc_ref, qt_ref, k_ref, vt_ref, o_ref, m_sc, acc_sc, *, nkv, online):
    j = pl.program_id(1)

    @pl.when(j == 0)
    def _():
        acc_sc[...] = jnp.zeros_like(acc_sc)
        for p in range(N_PAIR):
            if online:
                m_sc[p] = jnp.full(m_sc.shape[1:], -jnp.inf, F32)
            else:
                qf = qt_ref[p * DA_HEAD:(p + 1) * DA_HEAD, :].astype(F32)
                m_sc[p] = jnp.sqrt(jnp.sum(qf * qf, axis=0, keepdims=True)) * sc_ref[0, 1 + p]

    for p in range(N_PAIR):
        s = _dot(k_ref[p], qt_ref[p * DA_HEAD:(p + 1) * DA_HEAD, :])
        if online:
            m_old = m_sc[p]
            m_new = jnp.maximum(m_old, jnp.max(s, axis=0, keepdims=True))
            m_sc[p] = m_new
            acc_sc[p] = jnp.exp2(m_old - m_new) * acc_sc[p] + _dot(vt_ref[p // 2], jnp.exp2(s - m_new).astype(BF16))
        else:
            acc_sc[p] += _dot(vt_ref[p // 2], jnp.exp2(s - m_sc[p]).astype(BF16))

    @pl.when(j == nkv - 1)
    def _():
        lam = sc_ref[0, 0]
        heads = []
        for h in range(DA_HEADS):
            a0, a1 = acc_sc[2 * h], acc_sc[2 * h + 1]
            heads.append(a0[:DA_VDIM] / a0[DA_VDIM:DA_VDIM + 1] - lam * (a1[:DA_VDIM] / a1[DA_VDIM:DA_VDIM + 1]))
        o_ref[...] = jnp.concatenate(heads, axis=0).T


def _attention(scal, qt, k, vt, tq, tk, online):
    n_q = qt.shape[1]
    nq, nkv = n_q // tq, k.shape[1] // tk
    return pl.pallas_call(
        functools.partial(_attn_kernel, nkv=nkv, online=online),
        grid=(nq, nkv),
        in_specs=[pl.BlockSpec(memory_space=pltpu.SMEM),
                  pl.BlockSpec((256, tq), lambda i, j: (0, i)),
                  pl.BlockSpec((N_PAIR, tk, DA_HEAD), lambda i, j: (0, j, 0)),
                  pl.BlockSpec((DA_HEADS, ATT_VROWS, tk), lambda i, j: (0, 0, j))],
        out_specs=pl.BlockSpec((tq, DA_HEADS * DA_VDIM), lambda i, j: (i, 0)),
        out_shape=jax.ShapeDtypeStruct((n_q, DA_HEADS * DA_VDIM), F32),
        scratch_shapes=[pltpu.VMEM((N_PAIR, 1, tq), F32), pltpu.VMEM((N_PAIR, ATT_VROWS, tq), F32)],
        compiler_params=_params("arbitrary", "arbitrary"),
        name="diff_attn_online" if online else "diff_attn",
    )(scal, qt, k, vt)


def _gla_kernel(qkf_ref, qkr_ref, vf_ref, vr_ref, gaf_ref, gar_ref, wa_ref, ba_ref, tri_ref, sel_ref, bd_ref,
                of_ref, or_ref, s_sc):
    j = pl.program_id(0)

    @pl.when(j == 0)
    def _():
        s_sc[...] = jnp.zeros_like(s_sc)

    bd = bd_ref[...]
    stack = GLA_HEADS * GLA_CHUNK
    row_head = lax.broadcasted_iota(jnp.int32, (stack, W_GLA_K), 0) // GLA_CHUNK
    q_head = (lax.broadcasted_iota(jnp.int32, (stack, W_GLA_K), 1) // GLA_DK) == row_head
    ii = lax.broadcasted_iota(jnp.int32, (stack, GLA_CHUNK), 0) % GLA_CHUNK
    jj = lax.broadcasted_iota(jnp.int32, (stack, GLA_CHUNK), 1)
    lane_v = lax.broadcasted_iota(jnp.int32, (GLA_CHUNK, 256), 1) // GLA_DV
    nchunk = TILE // GLA_CHUNK
    states = [s_sc[0], s_sc[1]]
    for d, (rev, qk_ref, v_ref, ga_ref, o_ref) in enumerate(zip(DIRECTIONS, (qkf_ref, qkr_ref), (vf_ref, vr_ref),
                                                                (gaf_ref, gar_ref), (of_ref, or_ref))):
        q = qk_ref[:, :W_GLA_K] * (GLA_DK ** -0.5)
        k = qk_ref[:, W_GLA_K:]
        v = v_ref[...]
        ga = ga_ref[...]
        ga_hi = ga.astype(BF16)
        ga_lo = (ga - ga_hi.astype(F32)).astype(BF16)
        zg = _dot(ga_hi, wa_ref[d, 0]) + _dot(ga_lo, wa_ref[d, 0]) + _dot(ga_hi, wa_ref[d, 1])
        lg = jax.nn.log_sigmoid(zg + ba_ref[d]) / GLA_TAU
        lg_hi = lg.astype(BF16)
        lg_lo = (lg - lg_hi.astype(F32)).astype(BF16)
        b = _dot(tri_ref[d], lg_hi) + _dot(tri_ref[d], lg_lo)
        b_last = _dot(sel_ref[...], lg_hi) + _dot(sel_ref[...], lg_lo)
        qc = (q * jnp.exp(b)).astype(BF16)
        kinv_t = (k * jnp.exp(-b)).T.astype(BF16)
        kdec_t = (k * jnp.exp(b_last - b)).T.astype(BF16)
        decay_t = jnp.exp(b_last).T
        vb = v.astype(BF16)
        causal = (jj >= ii) if rev else (jj <= ii)
        s_state = states[d]
        for ci in range(nchunk):
            c = nchunk - 1 - ci if rev else ci
            rs = slice(c * GLA_CHUNK, (c + 1) * GLA_CHUNK)
            qcc, vc = qc[rs], vb[rs]
            q_stack = jnp.where(q_head, jnp.concatenate([qcc] * GLA_HEADS, axis=0), 0.0)
            att = jnp.where(causal, _dot(q_stack, kinv_t[:, rs]), 0.0)
            o_stack = _dot(att.astype(BF16), vc)
            o = _dot(qcc, s_state.astype(BF16))
            for h in range(GLA_HEADS):
                o = o + jnp.where(lane_v == h, o_stack[h * GLA_CHUNK:(h + 1) * GLA_CHUNK], 0.0)
            o_ref[rs, :] = o
            kv = _dot(kdec_t[:, rs], vc) * bd
            s_state = decay_t[:, c * GLA_CHUNK:c * GLA_CHUNK + 1] * s_state + kv
        states[d] = s_state
    s_sc[0], s_sc[1] = states


def _gla_scan(gqk, gv, ga, wa, ba, tri, sel, bd):
    t_all = gqk.shape[0]
    nt = t_all // TILE
    return pl.pallas_call(
        _gla_kernel,
        grid=(nt,),
        in_specs=(_dir_specs((TILE, 256), nt) + _dir_specs((TILE, 256), nt) + _dir_specs((TILE, 128), nt)
                  + [_const_spec((2, 2, 128, W_GLA_K)), _const_spec((2, 1, W_GLA_K)),
                     _const_spec((2, TILE, TILE)), _const_spec((TILE, TILE)), _const_spec((W_GLA_K, 256))]),
        out_specs=_dir_specs((TILE, 256), nt),
        out_shape=[jax.ShapeDtypeStruct((t_all, 256), F32)] * 2,
        scratch_shapes=[pltpu.VMEM((2, W_GLA_K, 256), F32)],
        compiler_params=_params("arbitrary"),
        name="gla_scan",
    )(gqk, gqk, gv, gv, ga, ga, wa, ba, tri, sel, bd)


def _merge_kernel(x_ref, mod_ref, u_ref, yf_ref, yb_ref, d_ref, wglu_ref, hf_ref, hb_ref, ly_ref,
                  att_ref, dan_ref, gf_ref, gb_ref, gg_ref, gn_ref, grp_ref, gates_ref, wb_ref, wo_ref,
                  o_ref, *, da_scale):
    z = jax.nn.gelu(u_ref[...] * d_ref[...] + yf_ref[...] + yb_ref[...])
    ya = z * jax.nn.sigmoid(_dot(z.astype(BF16), wglu_ref[...]))
    yb = (hf_ref[...] + hb_ref[...]) * jax.nn.gelu(ly_ref[...])
    og = gf_ref[...] + gb_ref[...]
    grp = grp_ref[...]
    gg = gg_ref[...]
    yd = og * lax.rsqrt(_group_mean(og * og, grp) + EPS) * gn_ref[...] * (gg * jax.nn.sigmoid(gg))
    att = att_ref[...]
    yc = att * lax.rsqrt(_group_mean(att * att, grp) + EPS) * (dan_ref[...] * da_scale)

    def gate(n):
        return jax.nn.sigmoid(gates_ref[:, n * D_MODEL:(n + 1) * D_MODEL])

    acc = gate(0) * _dot(ya.astype(BF16), wb_ref[0])
    for n, y in ((1, yb), (2, yc), (3, yd)):
        acc = acc + gate(n) * _dot(y.astype(BF16), wb_ref[n])
    o_ref[...] = x_ref[...] + mod_ref[0, 2:3] * _dot(acc.astype(BF16), wo_ref[...])


def _merge(xa, mods, u, yf, yb, ssm_d, w_glu, hf, hb, ly, att, da_norm, gf, gb, gg, gla_norm, grp64, gates,
           w_branch, w_out, da_scale):
    t_all = xa.shape[0]
    nt = t_all // TILE
    row = lambda w: pl.BlockSpec((TILE, w), lambda i: (i, 0))
    vec = lambda w: _const_spec((1, w))
    return pl.pallas_call(
        functools.partial(_merge_kernel, da_scale=da_scale),
        grid=(nt,),
        in_specs=[row(D_MODEL), pl.BlockSpec((1, SUBLANES, D_MODEL), lambda i: (jnp.minimum(i, 1), 0, 0)),
                  row(256), row(256), row(256), vec(256), _const_spec((256, 256)),
                  row(256), row(256), row(256),
                  row(256), vec(256),
                  row(256), row(256), row(256), vec(256), _const_spec((256, 256)), row(W_GATES),
                  _const_spec((N_BRANCH, W_BRANCH, D_MODEL)), _const_spec((D_MODEL, D_MODEL))],
        out_specs=row(D_MODEL),
        out_shape=jax.ShapeDtypeStruct((t_all, D_MODEL), F32),
        compiler_params=_params("arbitrary"),
        name="merge",
    )(xa, mods, u, yf, yb, ssm_d, w_glu, hf, hb, ly, att, da_norm, gf, gb, gg, gla_norm, grp64, gates,
      w_branch, w_out)


def _ffn_kernel(x_ref, xp_ref, xn_ref, mod_ref, g_ref, wa_ref, wg_ref, cw_ref, cb_ref, wd_ref, o_ref, ext, gt_sc,
                *, nt):
    i = pl.program_id(0)
    mod = mod_ref[0]
    g = g_ref[...]
    norm = lambda t: _rms(t, g) * (1.0 + mod[4:5]) + mod[3:4]
    prev_ok = (i >= 2).astype(F32)
    next_ok = jnp.logical_and(i != 0, i != nt - 1).astype(F32)
    x = x_ref[...]
    hn = norm(x)
    ext[0:HALO] = norm(xp_ref[...]) * prev_ok
    ext[HALO:HALO + TILE] = hn
    ext[HALO + TILE:] = norm(xn_ref[...]) * next_ok
    gt_sc[...] = _dot(ext[...].astype(BF16), wg_ref[...])
    gt = cb_ref[...] + sum(cw_ref[k:k + 1] * gt_sc[pl.ds(HALO - 1 + k, TILE), :] for k in range(3))
    a = _dot(hn.astype(BF16), wa_ref[...])
    y = _dot((jax.nn.gelu(gt) * a).astype(BF16), wd_ref[...])
    o_ref[...] = x + mod[5:6] * y


def _ffn(xa, mods, norm_g, w_up_a, w_up_g, conv_w, conv_b, w_down):
    t_all = xa.shape[0]
    nt = t_all // TILE
    per = TILE // HALO
    return pl.pallas_call(
        functools.partial(_ffn_kernel, nt=nt),
        grid=(nt,),
        in_specs=[pl.BlockSpec((TILE, D_MODEL), lambda i: (i, 0)),
                  pl.BlockSpec((HALO, D_MODEL), lambda i: (jnp.maximum(i * per - 1, 0), 0)),
                  pl.BlockSpec((HALO, D_MODEL), lambda i: (jnp.minimum((i + 1) * per, nt * per - 1), 0)),
                  pl.BlockSpec((1, SUBLANES, D_MODEL), lambda i: (jnp.minimum(i, 1), 0, 0)),
                  _const_spec((1, D_MODEL)), _const_spec((D_MODEL, D_FF)), _const_spec((D_MODEL, D_FF)),
                  _const_spec((3, D_FF)), _const_spec((1, D_FF)), _const_spec((D_FF, D_MODEL))],
        out_specs=pl.BlockSpec((TILE, D_MODEL), lambda i: (i, 0)),
        out_shape=jax.ShapeDtypeStruct((t_all, D_MODEL), F32),
        scratch_shapes=[pltpu.VMEM((TILE + 2 * HALO, D_MODEL), F32), pltpu.VMEM((TILE + 2 * HALO, D_FF), F32)],
        compiler_params=_params("arbitrary"),
        name="convffn",
    )(xa, xa, xa, mods, norm_g, w_up_a, w_up_g, conv_w, conv_b, w_down)


def _rope_tables(n_lat, n_ctx):
    pos = jnp.arange(n_lat)
    row, col = (pos // GRID_W).astype(F32), (pos % GRID_W).astype(F32)
    half = DA_HEAD // 2
    inv = 1.0 / (ROPE_BASE ** (jnp.arange(0, half, 2, dtype=F32) / half))
    ang = jnp.stack([row[:, None] * inv, col[:, None] * inv], axis=1)
    cos, sin = jnp.cos(ang), jnp.sin(ang)
    cos32 = jnp.stack([cos, cos], axis=2).reshape(n_lat, DA_HEAD)
    sin32 = jnp.stack([-sin, sin], axis=2).reshape(n_lat, DA_HEAD)
    cos_t = jnp.concatenate([jnp.ones((n_ctx, DA_HEAD), F32), cos32], axis=0)
    sin_t = jnp.concatenate([jnp.zeros((n_ctx, DA_HEAD), F32), sin32], axis=0)
    return jnp.tile(cos_t, (1, N_PAIR)), jnp.tile(sin_t, (1, N_PAIR))


def _group_mean_matrix(width, group):
    idx = jnp.arange(width) // group
    return ((idx[:, None] == idx[None, :]).astype(F32) / group).astype(BF16)


def _gla_chunk_matrices(rev):
    t = jnp.arange(TILE)
    same = (t[:, None] // GLA_CHUNK) == (t[None, :] // GLA_CHUNK)
    order = (t[None, :] >= t[:, None]) if rev else (t[None, :] <= t[:, None])
    return (same & order).astype(BF16), same.astype(BF16)


def kernel(x, c, ctx, c_ctx, w_ada, b_ada, norm1_g, norm2_g, w_in, ssm_lam_re, ssm_lam_im, ssm_log_step, ssm_b_re, ssm_b_im, ssm_c_re, ssm_c_im, ssm_d, ssm_w_glu, lru_conv_w, lru_conv_b, lru_wr, lru_br, lru_wi, lru_bi, lru_lam, da_q_norm, da_k_norm, da_lam, da_out_norm, gla_wa2, gla_ba, gla_out_norm, w_branch, w_out, w_up, ffn_conv_w, ffn_conv_b, w_down):
    depth = w_in.shape[0]
    n_lat, n_ctx = x.shape[1], ctx.shape[1]
    assert x.shape[0] == 1 and n_ctx == TILE and n_lat % TILE == 0
    xa = jnp.concatenate([ctx[0], x[0]], axis=0).astype(F32)

    cond = jnp.zeros((SUBLANES, D_MODEL), F32).at[0].set(c_ctx).at[1].set(c[0])
    ada = _adaln(cond, w_ada, b_ada)
    mods_all = jnp.pad(ada[:, :2].reshape(depth, 2, 6, D_MODEL), ((0, 0), (0, 0), (0, SUBLANES - 6), (0, 0)))

    cos_t, sin_t = _rope_tables(n_lat, n_ctx)
    grp32 = _group_mean_matrix(256, DA_HEAD)
    grp64 = _group_mean_matrix(256, GLA_DV)
    bd_mask = ((jnp.arange(W_GLA_K)[:, None] // GLA_DK) == (jnp.arange(256)[None, :] // GLA_DV)).astype(F32)
    gla_tri = jnp.stack([_gla_chunk_matrices(rev)[0] for rev in DIRECTIONS])
    gla_sel = _gla_chunk_matrices(False)[1]
    tk = TILE * max(d for d in range(1, ATT_TK_TILES + 1) if ((n_ctx + n_lat) // TILE) % d == 0)
    tq = min(ATT_TQ, n_lat)

    for l in range(depth):
        mods = mods_all[l]
        w_l = w_in[l]
        w_qkv = w_l[:, QKV_START:QKV_START + W_QKV].astype(BF16)
        w_main = jnp.concatenate([w_l[:, :QKV_START], w_l[:, QKV_START + W_QKV:W_MAIN]], axis=1).astype(BF16)
        w_a = jnp.pad(w_l[:, W_MAIN:W_MAIN + 2 * GLA_RANK], ((0, 0), (0, 128 - 2 * GLA_RANK))).astype(BF16)
        w_gates = w_l[:, W_MAIN + 2 * GLA_RANK:].astype(BF16)
        qg = jnp.tile(da_q_norm[l], N_PAIR)[None]
        kg = jnp.tile(da_k_norm[l], N_PAIR)[None]
        u, lx, ly, qt, k, vt, gqk, gv, gg, ga, gates, qmax2, kmax2 = _inproj(
            xa, mods, norm1_g[l][None], w_qkv, w_main, w_a, w_gates, cos_t, sin_t, qg, kg, grp32)

        s5_tabs = [_s5_tables(ssm_lam_re[l, di], ssm_lam_im[l, di], ssm_log_step[l, di], ssm_b_re[l, di],
                              ssm_b_im[l, di], ssm_c_re[l, di], ssm_c_im[l, di], rev)
                   for di, rev in enumerate(DIRECTIONS)]
        ys = _s5_scan(u, *(jnp.stack(t) for t in zip(*s5_tabs)))
        hs = _lru_scan(lx, lru_conv_w[l], lru_conv_b[l][None],
                       jnp.stack([_block_diag(lru_wr[l, di]) for di in range(2)]).astype(BF16), lru_br[l][:, None],
                       jnp.stack([_block_diag(lru_wi[l, di]) for di in range(2)]).astype(BF16), lru_bi[l][:, None],
                       lru_lam[l][:, None])
        wa = jnp.stack([jnp.zeros((128, W_GLA_K), F32).at[di * GLA_RANK:(di + 1) * GLA_RANK].set(gla_wa2[l, di])
                        for di in range(2)])
        wa_hi = wa.astype(BF16)
        wa_split = jnp.stack([wa_hi, (wa - wa_hi.astype(F32)).astype(BF16)], axis=1)
        gs = _gla_scan(gqk, gv, ga, wa_split, gla_ba[l][:, None], gla_tri, gla_sel, bd_mask)

        lp = da_lam[l].astype(F32)
        lam_init = 0.8 - 0.6 * math.exp(-0.3 * l)
        lam = jnp.exp(jnp.sum(lp[0] * lp[1])) - jnp.exp(jnp.sum(lp[2] * lp[3])) + lam_init
        qmax = jnp.sqrt(qmax2[0, ::DA_HEAD])
        kmax = jnp.sqrt(kmax2[0, ::DA_HEAD])
        scal = jnp.concatenate([lam.reshape(1), kmax, jnp.zeros((7,), F32)]).reshape(1, 16)

        def attend(online):
            att_lat = _attention(scal, qt[:, n_ctx:], k, vt, tq, tk, online)
            att_ctx = _attention(scal, qt[:, :n_ctx], k[:, :n_ctx], vt[:, :, :n_ctx], n_ctx, n_ctx, online)
            return att_ctx, att_lat

        safe = 2.0 * jnp.max(qmax * kmax) < MAX_SHIFT_SPREAD
        att_ctx, att_lat = lax.cond(safe, lambda: attend(False), lambda: attend(True))
        att = jnp.concatenate([att_ctx, att_lat], axis=0)

        xa = _merge(xa, mods, u, ys[0], ys[1], ssm_d[l][None], ssm_w_glu[l].astype(BF16), hs[0], hs[1], ly,
                    att, jnp.tile(da_out_norm[l], DA_HEADS)[None], gs[0], gs[1], gg,
                    jnp.tile(gla_out_norm[l], GLA_HEADS)[None],
                    grp64, gates, w_branch[l].astype(BF16), w_out[l].astype(BF16), 1.0 - lam_init)
        xa = _ffn(xa, mods, norm2_g[l][None], w_up[l][:, :D_FF].astype(BF16), w_up[l][:, D_FF:].astype(BF16),
                  ffn_conv_w[l], ffn_conv_b[l][None], w_down[l].astype(BF16))
    return xa[n_ctx:][None].astype(x.dtype)
```

```python
import functools
import math

import jax
import jax.numpy as jnp
from jax import lax
from jax.experimental import pallas as pl
from jax.experimental.pallas import tpu as pltpu

F32 = jnp.float32
BF16 = jnp.bfloat16
HIGHEST = lax.Precision.HIGHEST

D_MODEL = 1024
EPS = 1e-6
GRID_W = 64
TILE = 256
SUBLANES = 8
HALO = SUBLANES
VMEM_LIMIT = 56 * 1024 * 1024

SSM_GROUPS, SSM_GROUP, SSM_STATE = 16, 16, 64
W_SSM = 256
N_STATE = SSM_GROUPS * SSM_STATE
W_LRU, LRU_BLOCKS, LRU_BLOCK, LRU_C = 256, 8, 32, 8.0
DA_HEADS, DA_HEAD, DA_VDIM = 4, 32, 64
N_PAIR = 2 * DA_HEADS
ROPE_BASE = 10000.0
GLA_HEADS, GLA_DK, GLA_DV, GLA_RANK, GLA_TAU, GLA_CHUNK = 4, 32, 64, 16, 16.0, 64
W_GLA_K = GLA_HEADS * GLA_DK
N_BRANCH, W_BRANCH, D_FF = 4, 256, 2816
W_MAIN = 2304
QKV_START, W_QKV = 768, 768
W_GATES = N_BRANCH * D_MODEL
Q_SCALE = DA_HEAD ** -0.5 * math.log2(math.e)
MAX_SHIFT_SPREAD = 100.0
ATT_TQ = 1024
ATT_TK_TILES = 13
ATT_VROWS = 80


def _dot(a, b, precision=None):
    return jnp.dot(a, b, preferred_element_type=F32, precision=precision)


def _dot_nt(a, b, precision=None):
    return lax.dot_general(a, b, (((1,), (1,)), ((), ())), preferred_element_type=F32, precision=precision)


def _params(*sem):
    return pltpu.CompilerParams(dimension_semantics=sem, vmem_limit_bytes=VMEM_LIMIT)


def _const_spec(shape):
    nd = len(shape)
    return pl.BlockSpec(shape, lambda *_: (0,) * nd)


def _rms(x, g):
    return x * lax.rsqrt(jnp.mean(x * x, axis=-1, keepdims=True) + EPS) * g


def _group_mean(sq, grp):
    hi = sq.astype(BF16)
    lo = (sq - hi.astype(F32)).astype(BF16)
    return _dot(hi, grp) + _dot(lo, grp)


def _adaln_kernel(c_ref, w_ref, b_ref, o_ref):
    c = c_ref[...]
    o_ref[0] = _dot(c * jax.nn.sigmoid(c), w_ref[0], HIGHEST) + b_ref[0]


def _adaln(cond, w_ada, b_ada):
    depth, _, n_out = w_ada.shape
    blk = 1536
    return pl.pallas_call(
        _adaln_kernel,
        grid=(depth, n_out // blk),
        in_specs=[pl.BlockSpec((SUBLANES, D_MODEL), lambda l, n: (0, 0)),
                  pl.BlockSpec((1, D_MODEL, blk), lambda l, n: (l, 0, n)),
                  pl.BlockSpec((1, 1, blk), lambda l, n: (l, 0, n))],
        out_specs=pl.BlockSpec((1, SUBLANES, blk), lambda l, n: (l, 0, n)),
        out_shape=jax.ShapeDtypeStruct((depth, SUBLANES, n_out), F32),
        compiler_params=_params("arbitrary", "arbitrary"),
        name="adaln",
    )(cond, w_ada, b_ada.reshape(depth, 1, n_out))


def _inproj_kernel(x_ref, mod_ref, g_ref, wqkv_ref, wm_ref, wa_ref, wg_ref, cos_ref, sin_ref, qg_ref, kg_ref, grp_ref,
                   u_ref, lx_ref, ly_ref, qt_ref, k_ref, vt_ref, gqk_ref, gv_ref, gg_ref, ga_ref, gates_ref,
                   qmax_ref, kmax_ref):
    x = x_ref[...]
    mod = mod_ref[0]
    hn = _rms(x, g_ref[...]) * (1.0 + mod[1:2]) + mod[0:1]
    hb = hn.astype(BF16)
    zq = _dot(hb, wqkv_ref[...])

    cos, sin = cos_ref[...], sin_ref[...]
    lane = lax.broadcasted_iota(jnp.int32, (TILE, 256), 1)
    first_half = (lane % 16) < 8
    grp = grp_ref[...]

    def qk_norm_rope(t, g):
        tn = t * lax.rsqrt(_group_mean(t * t, grp) + EPS) * g
        partner = jnp.where(first_half, pltpu.roll(tn, 256 - 8, 1), pltpu.roll(tn, 8, 1))
        return tn * cos + partner * sin

    def max_sq_norm(tb):
        t = tb.astype(F32)
        return jnp.max(_dot((t * t).astype(BF16), grp) * DA_HEAD, axis=0, keepdims=True)

    @pl.when(pl.program_id(0) == 0)
    def _():
        qmax_ref[...] = jnp.zeros_like(qmax_ref)
        kmax_ref[...] = jnp.zeros_like(kmax_ref)

    qb = (qk_norm_rope(zq[:, 0:256], qg_ref[...]) * Q_SCALE).astype(BF16)
    kb = qk_norm_rope(zq[:, 256:512], kg_ref[...]).astype(BF16)
    qmax_ref[...] = jnp.maximum(qmax_ref[...], max_sq_norm(qb))
    kmax_ref[...] = jnp.maximum(kmax_ref[...], max_sq_norm(kb))
    qt_ref[...] = qb.T
    for p in range(N_PAIR):
        k_ref[p] = kb[:, p * DA_HEAD:(p + 1) * DA_HEAD]
    lane_v = lax.broadcasted_iota(jnp.int32, (TILE, 128), 1)
    tail = (lane_v == DA_VDIM).astype(F32)
    for h in range(DA_HEADS):
        vh = zq[:, 512 + 128 * (h // 2):512 + 128 * (h // 2 + 1)]
        if h % 2:
            vh = pltpu.roll(vh, DA_VDIM, 1)
        vt_ref[h] = jnp.where(lane_v < DA_VDIM, vh, tail).T[:ATT_VROWS].astype(BF16)

    z = _dot(hb, wm_ref[...])
    u_ref[...] = z[:, 0:256]
    lx_ref[...] = z[:, 256:512]
    ly_ref[...] = z[:, 512:768]
    gqk_ref[...] = z[:, 768:1024]
    gv_ref[...] = z[:, 1024:1280]
    gg_ref[...] = z[:, 1280:1536]
    ga_ref[...] = _dot(hb, wa_ref[...])
    gates_ref[...] = _dot(hb, wg_ref[...])


def _inproj(xa, mods, norm_g, w_qkv, w_main, w_a, w_gates, cos_t, sin_t, qg, kg, grp32):
    t_all = xa.shape[0]
    nt = t_all // TILE
    row = lambda w: pl.BlockSpec((TILE, w), lambda i: (i, 0))
    f = lambda w: jax.ShapeDtypeStruct((t_all, w), F32)
    return pl.pallas_call(
        _inproj_kernel,
        grid=(nt,),
        in_specs=[row(D_MODEL),
                  pl.BlockSpec((1, SUBLANES, D_MODEL), lambda i: (jnp.minimum(i, 1), 0, 0)),
                  _const_spec((1, D_MODEL)),
                  _const_spec((D_MODEL, W_QKV)), _const_spec((D_MODEL, W_MAIN - W_QKV)), _const_spec((D_MODEL, 128)),
                  _const_spec((D_MODEL, W_GATES)),
                  row(256), row(256), _const_spec((1, 256)), _const_spec((1, 256)), _const_spec((256, 256))],
        out_specs=[row(256), row(256), row(256),
                   pl.BlockSpec((256, TILE), lambda i: (0, i)),
                   pl.BlockSpec((N_PAIR, TILE, DA_HEAD), lambda i: (0, i, 0)),
                   pl.BlockSpec((DA_HEADS, ATT_VROWS, TILE), lambda i: (0, 0, i)),
                   row(256), row(256), row(256), row(128), row(W_GATES),
                   _const_spec((SUBLANES, 256)), _const_spec((SUBLANES, 256))],
        out_shape=[f(256), f(256), f(256), jax.ShapeDtypeStruct((256, t_all), BF16),
                   jax.ShapeDtypeStruct((N_PAIR, t_all, DA_HEAD), BF16),
                   jax.ShapeDtypeStruct((DA_HEADS, ATT_VROWS, t_all), BF16),
                   f(256), f(256), f(256), f(128), f(W_GATES),
                   jax.ShapeDtypeStruct((SUBLANES, 256), F32), jax.ShapeDtypeStruct((SUBLANES, 256), F32)],
        compiler_params=_params("arbitrary"),
        name="inproj",
    )(xa, mods, norm_g, w_qkv, w_main, w_a, w_gates, cos_t, sin_t, qg, kg, grp32)


def _scan_tile_index(j, nt, rev):
    return jnp.where(j == 0, 0, nt - j) if rev else j


DIRECTIONS = (False, True)


def _s5_kernel(uf_ref, ur_ref, b_ref, c_ref, tab_ref, yf_ref, yr_ref, hre, him, car):
    j = pl.program_id(0)

    @pl.when(j == 0)
    def _():
        car[...] = jnp.zeros_like(car)

    for d, u_ref in enumerate((uf_ref, ur_ref)):
        bu = _dot(u_ref[...].astype(BF16), b_ref[d])
        hre[d] = bu[:, :N_STATE]
        him[d] = bu[:, N_STATE:]
    nblk = TILE // SUBLANES

    def body(b, carry):
        out = []
        for d, rev in enumerate(DIRECTIONS):
            cr, ci = carry[d]
            blk = nblk - 1 - b if rev else b
            r0 = pl.multiple_of(blk * SUBLANES, SUBLANES)
            xr = hre[d, pl.ds(r0, SUBLANES), :]
            xi = him[d, pl.ds(r0, SUBLANES), :]
            for k, s in enumerate((1, 2, 4)):
                shift = SUBLANES - s if rev else s
                sr = pltpu.roll(xr, shift, 0)
                si = pltpu.roll(xi, shift, 0)
                mr, mi = tab_ref[d, 2 * k], tab_ref[d, 2 * k + 1]
                xr, xi = xr + mr * sr - mi * si, xi + mr * si + mi * sr
            pr, pi_ = tab_ref[d, 6], tab_ref[d, 7]
            xr, xi = xr + pr * cr - pi_ * ci, xi + pr * ci + pi_ * cr
            hre[d, pl.ds(r0, SUBLANES), :] = xr
            him[d, pl.ds(r0, SUBLANES), :] = xi
            last = 0 if rev else SUBLANES - 1
            out.append((jnp.broadcast_to(xr[last:last + 1], xr.shape), jnp.broadcast_to(xi[last:last + 1], xi.shape)))
        return tuple(out)

    carry = lax.fori_loop(0, nblk, body, tuple((car[d, 0], car[d, 1]) for d in range(2)))
    for d, y_ref in enumerate((yf_ref, yr_ref)):
        car[d, 0], car[d, 1] = carry[d]
        y_ref[...] = (_dot(hre[d].astype(BF16), c_ref[d, :N_STATE]) + _dot(him[d].astype(BF16), c_ref[d, N_STATE:]))


def _dir_specs(shape, nt, index=lambda ti: (ti, 0)):
    return [pl.BlockSpec(shape, functools.partial(lambda j, rev: index(_scan_tile_index(j, nt, rev)), rev=rev))
            for rev in DIRECTIONS]


def _s5_scan(u, bcat, ccat, tabs):
    t_all = u.shape[0]
    nt = t_all // TILE
    return pl.pallas_call(
        _s5_kernel,
        grid=(nt,),
        in_specs=_dir_specs((TILE, W_SSM), nt) + [_const_spec((2, W_SSM, 2 * N_STATE)),
                                                   _const_spec((2, 2 * N_STATE, W_SSM)),
                                                   _const_spec((2, 8, SUBLANES, N_STATE))],
        out_specs=_dir_specs((TILE, W_SSM), nt),
        out_shape=[jax.ShapeDtypeStruct((t_all, W_SSM), F32)] * 2,
        scratch_shapes=[pltpu.VMEM((2, TILE, N_STATE), F32), pltpu.VMEM((2, TILE, N_STATE), F32),
                        pltpu.VMEM((2, 2, SUBLANES, N_STATE), F32)],
        compiler_params=_params("arbitrary"),
        name="s5_scan",
    )(u, u, bcat, ccat, tabs)


def _s5_tables(lam_re, lam_im, log_step, b_re, b_im, c_re, c_im, rev):
    dt = jnp.exp(log_step.astype(F32))[:, None]
    lre, lim = lam_re.astype(F32) * dt, lam_im.astype(F32) * dt
    mag = jnp.exp(lre)
    lb_re, lb_im = mag * jnp.cos(lim), mag * jnp.sin(lim)
    num_re, num_im = lb_re - 1.0, lb_im
    den = lam_re * lam_re + lam_im * lam_im
    k_re = (num_re * lam_re + num_im * lam_im) / den
    k_im = (num_im * lam_re - num_re * lam_im) / den
    bb_re = k_re[..., None] * b_re - k_im[..., None] * b_im
    bb_im = k_re[..., None] * b_im + k_im[..., None] * b_re
    eye = jnp.eye(SSM_GROUPS, dtype=F32)
    blk_in = lambda t: jnp.einsum('gpc,gh->gchp', t, eye).reshape(W_SSM, N_STATE)
    blk_out = lambda t: jnp.einsum('gcp,gh->gphc', t, eye).reshape(N_STATE, W_SSM)
    bcat = jnp.concatenate([blk_in(bb_re), blk_in(bb_im)], axis=1).astype(BF16)
    ccat = jnp.concatenate([blk_out(c_re.astype(F32)), -blk_out(c_im.astype(F32))], axis=0).astype(BF16)

    def power(k):
        k = k[:, None].astype(F32)
        m = jnp.exp(k * lre.reshape(1, -1))
        return m * jnp.cos(k * lim.reshape(1, -1)), m * jnp.sin(k * lim.reshape(1, -1))

    r = jnp.arange(SUBLANES)
    tabs = []
    for s in (1, 2, 4):
        pr, pi_ = power(jnp.full((SUBLANES,), s))
        keep = ((r + s <= SUBLANES - 1) if rev else (r >= s))[:, None]
        tabs += [jnp.where(keep, pr, 0.0), jnp.where(keep, pi_, 0.0)]
    tabs += list(power((SUBLANES - r) if rev else (r + 1)))
    return bcat, ccat, jnp.stack(tabs)


def _lru_kernel(xf_ref, xr_ref, xpf_ref, xpr_ref, xnf_ref, xnr_ref, cw_ref, cb_ref, wr_ref, br_ref, wi_ref, bi_ref,
                lam_ref, hf_ref, hr_ref, ext, a_sc, b_sc, car, *, nt):
    j = pl.program_id(0)

    @pl.when(j == 0)
    def _():
        car[...] = jnp.zeros_like(car)

    for d, (rev, x_ref, xp_ref, xn_ref) in enumerate(zip(DIRECTIONS, (xf_ref, xr_ref), (xpf_ref, xpr_ref),
                                                         (xnf_ref, xnr_ref))):
        ti = _scan_tile_index(j, nt, rev)
        prev_ok = (ti >= 2).astype(F32)
        next_ok = jnp.logical_and(ti != 0, ti != nt - 1).astype(F32)
        ext[d, 0:HALO] = xp_ref[...] * prev_ok
        ext[d, HALO:HALO + TILE] = x_ref[...]
        ext[d, HALO + TILE:] = xn_ref[...] * next_ok
        xc = cb_ref[...] + sum(cw_ref[k:k + 1] * ext[d, pl.ds(HALO - 2 + k, TILE), :] for k in range(4))
        xb = xc.astype(BF16)
        r = jax.nn.sigmoid(_dot(xb, wr_ref[d]) + br_ref[d])
        i = jax.nn.sigmoid(_dot(xb, wi_ref[d]) + bi_ref[d])
        log_a = LRU_C * r * jax.nn.log_sigmoid(lam_ref[d])
        a = jnp.exp(log_a)
        a_sc[d] = a
        b_sc[d] = jnp.sqrt(-jnp.tanh(log_a) * (a * a + 1.0)) * (i * xc)
    nblk = TILE // SUBLANES
    rows = lax.broadcasted_iota(jnp.int32, (SUBLANES, W_LRU), 0)

    def body(b, carry):
        out = []
        for d, rev in enumerate(DIRECTIONS):
            blk = nblk - 1 - b if rev else b
            r0 = pl.multiple_of(blk * SUBLANES, SUBLANES)
            a = a_sc[d, pl.ds(r0, SUBLANES), :]
            x = b_sc[d, pl.ds(r0, SUBLANES), :]
            for s in (1, 2, 4):
                shift = SUBLANES - s if rev else s
                keep = (rows + s <= SUBLANES - 1) if rev else (rows >= s)
                a_s = jnp.where(keep, pltpu.roll(a, shift, 0), 1.0)
                x_s = jnp.where(keep, pltpu.roll(x, shift, 0), 0.0)
                a, x = a * a_s, a * x_s + x
            h = x + a * carry[d]
            b_sc[d, pl.ds(r0, SUBLANES), :] = h
            last = 0 if rev else SUBLANES - 1
            out.append(jnp.broadcast_to(h[last:last + 1], h.shape))
        return tuple(out)

    carry = lax.fori_loop(0, nblk, body, (car[0], car[1]))
    for d, h_ref in enumerate((hf_ref, hr_ref)):
        car[d] = carry[d]
        h_ref[...] = b_sc[d]


def _lru_scan(lx, conv_w, conv_b, wr, br, wi, bi, lam):
    t_all = lx.shape[0]
    nt = t_all // TILE
    per = TILE // HALO
    vec = _const_spec((2, 1, W_LRU))
    return pl.pallas_call(
        functools.partial(_lru_kernel, nt=nt),
        grid=(nt,),
        in_specs=(_dir_specs((TILE, W_LRU), nt)
                  + _dir_specs((HALO, W_LRU), nt, lambda ti: (jnp.maximum(ti * per - 1, 0), 0))
                  + _dir_specs((HALO, W_LRU), nt, lambda ti: (jnp.minimum((ti + 1) * per, nt * per - 1), 0))
                  + [_const_spec((4, W_LRU)), _const_spec((1, W_LRU)), _const_spec((2, W_LRU, W_LRU)), vec,
                     _const_spec((2, W_LRU, W_LRU)), vec, vec]),
        out_specs=_dir_specs((TILE, W_LRU), nt),
        out_shape=[jax.ShapeDtypeStruct((t_all, W_LRU), F32)] * 2,
        scratch_shapes=[pltpu.VMEM((2, TILE + 2 * HALO, W_LRU), F32), pltpu.VMEM((2, TILE, W_LRU), F32),
                        pltpu.VMEM((2, TILE, W_LRU), F32), pltpu.VMEM((2, SUBLANES, W_LRU), F32)],
        compiler_params=_params("arbitrary"),
        name="lru_scan",
    )(lx, lx, lx, lx, lx, lx, conv_w, conv_b, wr, br, wi, bi, lam)


def _block_diag(w):
    n, a, b = w.shape
    return jnp.einsum('nab,nm->namb', w, jnp.eye(n, dtype=w.dtype)).reshape(n * a, n * b)


def _attn_kernel(sc_ref, qt_ref, k_ref, vt_ref, o_ref, m_sc, acc_sc, *, nkv, online):
    j = pl.program_id(1)

    @pl.when(j == 0)
    def _():
        acc_sc[...] = jnp.zeros_like(acc_sc)
        for p in range(N_PAIR):
            if online:
                m_sc[p] = jnp.full(m_sc.shape[1:], -jnp.inf, F32)
            else:
                qf = qt_ref[p * DA_HEAD:(p + 1) * DA_HEAD, :].astype(F32)
                m_sc[p] = jnp.sqrt(jnp.sum(qf * qf, axis=0, keepdims=True)) * sc_ref[0, 1 + p]

    for p in range(N_PAIR):
        s = _dot(k_ref[p], qt_ref[p * DA_HEAD:(p + 1) * DA_HEAD, :])
        if online:
            m_old = m_sc[p]
            m_new = jnp.maximum(m_old, jnp.max(s, axis=0, keepdims=True))
            m_sc[p] = m_new
            acc_sc[p] = jnp.exp2(m_old - m_new) * acc_sc[p] + _dot(vt_ref[p // 2], jnp.exp2(s - m_new).astype(BF16))
        else:
            acc_sc[p] += _dot(vt_ref[p // 2], jnp.exp2(s - m_sc[p]).astype(BF16))

    @pl.when(j == nkv - 1)
    def _():
        lam = sc_ref[0, 0]
        heads = []
        for h in range(DA_HEADS):
            a0, a1 = acc_sc[2 * h], acc_sc[2 * h + 1]
            heads.append(a0[:DA_VDIM] / a0[DA_VDIM:DA_VDIM + 1] - lam * (a1[:DA_VDIM] / a1[DA_VDIM:DA_VDIM + 1]))
        o_ref[...] = jnp.concatenate(heads, axis=0).T


def _attention(scal, qt, k, vt, tq, tk, online):
    n_q = qt.shape[1]
    nq, nkv = n_q // tq, k.shape[1] // tk
    return pl.pallas_call(
        functools.partial(_attn_kernel, nkv=nkv, online=online),
        grid=(nq, nkv),
        in_specs=[pl.BlockSpec(memory_space=pltpu.SMEM),
                  pl.BlockSpec((256, tq), lambda i, j: (0, i)),
                  pl.BlockSpec((N_PAIR, tk, DA_HEAD), lambda i, j: (0, j, 0)),
                  pl.BlockSpec((DA_HEADS, ATT_VROWS, tk), lambda i, j: (0, 0, j))],
        out_specs=pl.BlockSpec((tq, DA_HEADS * DA_VDIM), lambda i, j: (i, 0)),
        out_shape=jax.ShapeDtypeStruct((n_q, DA_HEADS * DA_VDIM), F32),
        scratch_shapes=[pltpu.VMEM((N_PAIR, 1, tq), F32), pltpu.VMEM((N_PAIR, ATT_VROWS, tq), F32)],
        compiler_params=_params("arbitrary", "arbitrary"),
        name="diff_attn_online" if online else "diff_attn",
    )(scal, qt, k, vt)


def _gla_kernel(qkf_ref, qkr_ref, vf_ref, vr_ref, gaf_ref, gar_ref, wa_ref, ba_ref, tri_ref, sel_ref, bd_ref,
                of_ref, or_ref, s_sc):
    j = pl.program_id(0)

    @pl.when(j == 0)
    def _():
        s_sc[...] = jnp.zeros_like(s_sc)

    bd = bd_ref[...]
    stack = GLA_HEADS * GLA_CHUNK
    row_head = lax.broadcasted_iota(jnp.int32, (stack, W_GLA_K), 0) // GLA_CHUNK
    q_head = (lax.broadcasted_iota(jnp.int32, (stack, W_GLA_K), 1) // GLA_DK) == row_head
    ii = lax.broadcasted_iota(jnp.int32, (stack, GLA_CHUNK), 0) % GLA_CHUNK
    jj = lax.broadcasted_iota(jnp.int32, (stack, GLA_CHUNK), 1)
    lane_v = lax.broadcasted_iota(jnp.int32, (GLA_CHUNK, 256), 1) // GLA_DV
    nchunk = TILE // GLA_CHUNK
    states = [s_sc[0], s_sc[1]]
    for d, (rev, qk_ref, v_ref, ga_ref, o_ref) in enumerate(zip(DIRECTIONS, (qkf_ref, qkr_ref), (vf_ref, vr_ref),
                                                                (gaf_ref, gar_ref), (of_ref, or_ref))):
        q = qk_ref[:, :W_GLA_K] * (GLA_DK ** -0.5)
        k = qk_ref[:, W_GLA_K:]
        v = v_ref[...]
        ga = ga_ref[...]
        ga_hi = ga.astype(BF16)
        ga_lo = (ga - ga_hi.astype(F32)).astype(BF16)
        zg = _dot(ga_hi, wa_ref[d, 0]) + _dot(ga_lo, wa_ref[d, 0]) + _dot(ga_hi, wa_ref[d, 1])
        lg = jax.nn.log_sigmoid(zg + ba_ref[d]) / GLA_TAU
        lg_hi = lg.astype(BF16)
        lg_lo = (lg - lg_hi.astype(F32)).astype(BF16)
        b = _dot(tri_ref[d], lg_hi) + _dot(tri_ref[d], lg_lo)
        b_last = _dot(sel_ref[...], lg_hi) + _dot(sel_ref[...], lg_lo)
        qc = (q * jnp.exp(b)).astype(BF16)
        kinv_t = (k * jnp.exp(-b)).T.astype(BF16)
        kdec_t = (k * jnp.exp(b_last - b)).T.astype(BF16)
        decay_t = jnp.exp(b_last).T
        vb = v.astype(BF16)
        causal = (jj >= ii) if rev else (jj <= ii)
        s_state = states[d]
        for ci in range(nchunk):
            c = nchunk - 1 - ci if rev else ci
            rs = slice(c * GLA_CHUNK, (c + 1) * GLA_CHUNK)
            qcc, vc = qc[rs], vb[rs]
            q_stack = jnp.where(q_head, jnp.concatenate([qcc] * GLA_HEADS, axis=0), 0.0)
            att = jnp.where(causal, _dot(q_stack, kinv_t[:, rs]), 0.0)
            o_stack = _dot(att.astype(BF16), vc)
            o = _dot(qcc, s_state.astype(BF16))
            for h in range(GLA_HEADS):
                o = o + jnp.where(lane_v == h, o_stack[h * GLA_CHUNK:(h + 1) * GLA_CHUNK], 0.0)
            o_ref[rs, :] = o
            kv = _dot(kdec_t[:, rs], vc) * bd
            s_state = decay_t[:, c * GLA_CHUNK:c * GLA_CHUNK + 1] * s_state + kv
        states[d] = s_state
    s_sc[0], s_sc[1] = states


def _gla_scan(gqk, gv, ga, wa, ba, tri, sel, bd):
    t_all = gqk.shape[0]
    nt = t_all // TILE
    return pl.pallas_call(
        _gla_kernel,
        grid=(nt,),
        in_specs=(_dir_specs((TILE, 256), nt) + _dir_specs((TILE, 256), nt) + _dir_specs((TILE, 128), nt)
                  + [_const_spec((2, 2, 128, W_GLA_K)), _const_spec((2, 1, W_GLA_K)),
                     _const_spec((2, TILE, TILE)), _const_spec((TILE, TILE)), _const_spec((W_GLA_K, 256))]),
        out_specs=_dir_specs((TILE, 256), nt),
        out_shape=[jax.ShapeDtypeStruct((t_all, 256), F32)] * 2,
        scratch_shapes=[pltpu.VMEM((2, W_GLA_K, 256), F32)],
        compiler_params=_params("arbitrary"),
        name="gla_scan",
    )(gqk, gqk, gv, gv, ga, ga, wa, ba, tri, sel, bd)


def _merge_kernel(x_ref, mod_ref, u_ref, yf_ref, yb_ref, d_ref, wglu_ref, hf_ref, hb_ref, ly_ref,
                  att_ref, dan_ref, gf_ref, gb_ref, gg_ref, gn_ref, grp_ref, gates_ref, wb_ref, wo_ref,
                  o_ref, *, da_scale):
    z = jax.nn.gelu(u_ref[...] * d_ref[...] + yf_ref[...] + yb_ref[...])
    ya = z * jax.nn.sigmoid(_dot(z.astype(BF16), wglu_ref[...]))
    yb = (hf_ref[...] + hb_ref[...]) * jax.nn.gelu(ly_ref[...])
    og = gf_ref[...] + gb_ref[...]
    grp = grp_ref[...]
    gg = gg_ref[...]
    yd = og * lax.rsqrt(_group_mean(og * og, grp) + EPS) * gn_ref[...] * (gg * jax.nn.sigmoid(gg))
    att = att_ref[...]
    yc = att * lax.rsqrt(_group_mean(att * att, grp) + EPS) * (dan_ref[...] * da_scale)

    def gate(n):
        return jax.nn.sigmoid(gates_ref[:, n * D_MODEL:(n + 1) * D_MODEL])

    acc = gate(0) * _dot(ya.astype(BF16), wb_ref[0])
    for n, y in ((1, yb), (2, yc), (3, yd)):
        acc = acc + gate(n) * _dot(y.astype(BF16), wb_ref[n])
    o_ref[...] = x_ref[...] + mod_ref[0, 2:3] * _dot(acc.astype(BF16), wo_ref[...])


def _merge(xa, mods, u, yf, yb, ssm_d, w_glu, hf, hb, ly, att, da_norm, gf, gb, gg, gla_norm, grp64, gates,
           w_branch, w_out, da_scale):
    t_all = xa.shape[0]
    nt = t_all // TILE
    row = lambda w: pl.BlockSpec((TILE, w), lambda i: (i, 0))
    vec = lambda w: _const_spec((1, w))
    return pl.pallas_call(
        functools.partial(_merge_kernel, da_scale=da_scale),
        grid=(nt,),
        in_specs=[row(D_MODEL), pl.BlockSpec((1, SUBLANES, D_MODEL), lambda i: (jnp.minimum(i, 1), 0, 0)),
                  row(256), row(256), row(256), vec(256), _const_spec((256, 256)),
                  row(256), row(256), row(256),
                  row(256), vec(256),
                  row(256), row(256), row(256), vec(256), _const_spec((256, 256)), row(W_GATES),
                  _const_spec((N_BRANCH, W_BRANCH, D_MODEL)), _const_spec((D_MODEL, D_MODEL))],
        out_specs=row(D_MODEL),
        out_shape=jax.ShapeDtypeStruct((t_all, D_MODEL), F32),
        compiler_params=_params("arbitrary"),
        name="merge",
    )(xa, mods, u, yf, yb, ssm_d, w_glu, hf, hb, ly, att, da_norm, gf, gb, gg, gla_norm, grp64, gates,
      w_branch, w_out)


def _ffn_kernel(x_ref, xp_ref, xn_ref, mod_ref, g_ref, wa_ref, wg_ref, cw_ref, cb_ref, wd_ref, o_ref, ext, gt_sc,
                *, nt):
    i = pl.program_id(0)
    mod = mod_ref[0]
    g = g_ref[...]
    norm = lambda t: _rms(t, g) * (1.0 + mod[4:5]) + mod[3:4]
    prev_ok = (i >= 2).astype(F32)
    next_ok = jnp.logical_and(i != 0, i != nt - 1).astype(F32)
    x = x_ref[...]
    hn = norm(x)
    ext[0:HALO] = norm(xp_ref[...]) * prev_ok
    ext[HALO:HALO + TILE] = hn
    ext[HALO + TILE:] = norm(xn_ref[...]) * next_ok
    gt_sc[...] = _dot(ext[...].astype(BF16), wg_ref[...])
    gt = cb_ref[...] + sum(cw_ref[k:k + 1] * gt_sc[pl.ds(HALO - 1 + k, TILE), :] for k in range(3))
    a = _dot(hn.astype(BF16), wa_ref[...])
    y = _dot((jax.nn.gelu(gt) * a).astype(BF16), wd_ref[...])
    o_ref[...] = x + mod[5:6] * y


def _ffn(xa, mods, norm_g, w_up_a, w_up_g, conv_w, conv_b, w_down, latents_only):
    t_all = xa.shape[0]
    nt = t_all // TILE
    per = TILE // HALO
    if latents_only:
        out_rows, out_index = t_all - TILE, lambda i: (jnp.maximum(i - 1, 0), 0)
    else:
        out_rows, out_index = t_all, lambda i: (i, 0)
    return pl.pallas_call(
        functools.partial(_ffn_kernel, nt=nt),
        grid=(nt,),
        in_specs=[pl.BlockSpec((TILE, D_MODEL), lambda i: (i, 0)),
                  pl.BlockSpec((HALO, D_MODEL), lambda i: (jnp.maximum(i * per - 1, 0), 0)),
                  pl.BlockSpec((HALO, D_MODEL), lambda i: (jnp.minimum((i + 1) * per, nt * per - 1), 0)),
                  pl.BlockSpec((1, SUBLANES, D_MODEL), lambda i: (jnp.minimum(i, 1), 0, 0)),
                  _const_spec((1, D_MODEL)), _const_spec((D_MODEL, D_FF)), _const_spec((D_MODEL, D_FF)),
                  _const_spec((3, D_FF)), _const_spec((1, D_FF)), _const_spec((D_FF, D_MODEL))],
        out_specs=pl.BlockSpec((TILE, D_MODEL), out_index),
        out_shape=jax.ShapeDtypeStruct((out_rows, D_MODEL), F32),
        scratch_shapes=[pltpu.VMEM((TILE + 2 * HALO, D_MODEL), F32), pltpu.VMEM((TILE + 2 * HALO, D_FF), F32)],
        compiler_params=_params("arbitrary"),
        name="convffn",
    )(xa, xa, xa, mods, norm_g, w_up_a, w_up_g, conv_w, conv_b, w_down)


def _rope_tables(n_lat, n_ctx):
    pos = jnp.arange(n_lat)
    row, col = (pos // GRID_W).astype(F32), (pos % GRID_W).astype(F32)
    half = DA_HEAD // 2
    inv = 1.0 / (ROPE_BASE ** (jnp.arange(0, half, 2, dtype=F32) / half))
    ang = jnp.stack([row[:, None] * inv, col[:, None] * inv], axis=1)
    cos, sin = jnp.cos(ang), jnp.sin(ang)
    cos32 = jnp.stack([cos, cos], axis=2).reshape(n_lat, DA_HEAD)
    sin32 = jnp.stack([-sin, sin], axis=2).reshape(n_lat, DA_HEAD)
    cos_t = jnp.concatenate([jnp.ones((n_ctx, DA_HEAD), F32), cos32], axis=0)
    sin_t = jnp.concatenate([jnp.zeros((n_ctx, DA_HEAD), F32), sin32], axis=0)
    return jnp.tile(cos_t, (1, N_PAIR)), jnp.tile(sin_t, (1, N_PAIR))


def _group_mean_matrix(width, group):
    idx = jnp.arange(width) // group
    return ((idx[:, None] == idx[None, :]).astype(F32) / group).astype(BF16)


def _gla_chunk_matrices(rev):
    t = jnp.arange(TILE)
    same = (t[:, None] // GLA_CHUNK) == (t[None, :] // GLA_CHUNK)
    order = (t[None, :] >= t[:, None]) if rev else (t[None, :] <= t[:, None])
    return (same & order).astype(BF16), same.astype(BF16)


def kernel(x, c, ctx, c_ctx, w_ada, b_ada, norm1_g, norm2_g, w_in, ssm_lam_re, ssm_lam_im, ssm_log_step, ssm_b_re, ssm_b_im, ssm_c_re, ssm_c_im, ssm_d, ssm_w_glu, lru_conv_w, lru_conv_b, lru_wr, lru_br, lru_wi, lru_bi, lru_lam, da_q_norm, da_k_norm, da_lam, da_out_norm, gla_wa2, gla_ba, gla_out_norm, w_branch, w_out, w_up, ffn_conv_w, ffn_conv_b, w_down):
    depth = w_in.shape[0]
    n_lat, n_ctx = x.shape[1], ctx.shape[1]
    assert x.shape[0] == 1 and n_ctx == TILE and n_lat % TILE == 0
    xa = jnp.concatenate([ctx[0], x[0]], axis=0).astype(F32)

    cond = jnp.zeros((SUBLANES, D_MODEL), F32).at[0].set(c_ctx).at[1].set(c[0])
    ada = _adaln(cond, w_ada, b_ada)
    mods_all = jnp.pad(ada[:, :2].reshape(depth, 2, 6, D_MODEL), ((0, 0), (0, 0), (0, SUBLANES - 6), (0, 0)))

    cos_t, sin_t = _rope_tables(n_lat, n_ctx)
    grp32 = _group_mean_matrix(256, DA_HEAD)
    grp64 = _group_mean_matrix(256, GLA_DV)
    bd_mask = ((jnp.arange(W_GLA_K)[:, None] // GLA_DK) == (jnp.arange(256)[None, :] // GLA_DV)).astype(F32)
    gla_tri = jnp.stack([_gla_chunk_matrices(rev)[0] for rev in DIRECTIONS])
    gla_sel = _gla_chunk_matrices(False)[1]
    tk = TILE * max(d for d in range(1, ATT_TK_TILES + 1) if ((n_ctx + n_lat) // TILE) % d == 0)
    tq = min(ATT_TQ, n_lat)

    for l in range(depth):
        mods = mods_all[l]
        w_l = w_in[l]
        w_qkv = w_l[:, QKV_START:QKV_START + W_QKV].astype(BF16)
        w_main = jnp.concatenate([w_l[:, :QKV_START], w_l[:, QKV_START + W_QKV:W_MAIN]], axis=1).astype(BF16)
        w_a = jnp.pad(w_l[:, W_MAIN:W_MAIN + 2 * GLA_RANK], ((0, 0), (0, 128 - 2 * GLA_RANK))).astype(BF16)
        w_gates = w_l[:, W_MAIN + 2 * GLA_RANK:].astype(BF16)
        qg = jnp.tile(da_q_norm[l], N_PAIR)[None]
        kg = jnp.tile(da_k_norm[l], N_PAIR)[None]
        u, lx, ly, qt, k, vt, gqk, gv, gg, ga, gates, qmax2, kmax2 = _inproj(
            xa, mods, norm1_g[l][None], w_qkv, w_main, w_a, w_gates, cos_t, sin_t, qg, kg, grp32)

        s5_tabs = [_s5_tables(ssm_lam_re[l, di], ssm_lam_im[l, di], ssm_log_step[l, di], ssm_b_re[l, di],
                              ssm_b_im[l, di], ssm_c_re[l, di], ssm_c_im[l, di], rev)
                   for di, rev in enumerate(DIRECTIONS)]
        ys = _s5_scan(u, *(jnp.stack(t) for t in zip(*s5_tabs)))
        hs = _lru_scan(lx, lru_conv_w[l], lru_conv_b[l][None],
                       jnp.stack([_block_diag(lru_wr[l, di]) for di in range(2)]).astype(BF16), lru_br[l][:, None],
                       jnp.stack([_block_diag(lru_wi[l, di]) for di in range(2)]).astype(BF16), lru_bi[l][:, None],
                       lru_lam[l][:, None])
        wa = jnp.stack([jnp.zeros((128, W_GLA_K), F32).at[di * GLA_RANK:(di + 1) * GLA_RANK].set(gla_wa2[l, di])
                        for di in range(2)])
        wa_hi = wa.astype(BF16)
        wa_split = jnp.stack([wa_hi, (wa - wa_hi.astype(F32)).astype(BF16)], axis=1)
        gs = _gla_scan(gqk, gv, ga, wa_split, gla_ba[l][:, None], gla_tri, gla_sel, bd_mask)

        lp = da_lam[l].astype(F32)
        lam_init = 0.8 - 0.6 * math.exp(-0.3 * l)
        lam = jnp.exp(jnp.sum(lp[0] * lp[1])) - jnp.exp(jnp.sum(lp[2] * lp[3])) + lam_init
        qmax = jnp.sqrt(qmax2[0, ::DA_HEAD])
        kmax = jnp.sqrt(kmax2[0, ::DA_HEAD])
        scal = jnp.concatenate([lam.reshape(1), kmax, jnp.zeros((7,), F32)]).reshape(1, 16)

        def attend(online):
            att_lat = _attention(scal, qt[:, n_ctx:], k, vt, tq, tk, online)
            att_ctx = _attention(scal, qt[:, :n_ctx], k[:, :n_ctx], vt[:, :, :n_ctx], n_ctx, n_ctx, online)
            return att_ctx, att_lat

        safe = 2.0 * jnp.max(qmax * kmax) < MAX_SHIFT_SPREAD
        att_ctx, att_lat = lax.cond(safe, lambda: attend(False), lambda: attend(True))
        att = jnp.concatenate([att_ctx, att_lat], axis=0)

        xa = _merge(xa, mods, u, ys[0], ys[1], ssm_d[l][None], ssm_w_glu[l].astype(BF16), hs[0], hs[1], ly,
                    att, jnp.tile(da_out_norm[l], DA_HEADS)[None], gs[0], gs[1], gg,
                    jnp.tile(gla_out_norm[l], GLA_HEADS)[None],
                    grp64, gates, w_branch[l].astype(BF16), w_out[l].astype(BF16), 1.0 - lam_init)
        xa = _ffn(xa, mods, norm2_g[l][None], w_up[l][:, :D_FF].astype(BF16), w_up[l][:, D_FF:].astype(BF16),
                  ffn_conv_w[l], ffn_conv_b[l][None], w_down[l].astype(BF16), latents_only=(l == depth - 1))
    return xa[None].astype(x.dtype)
```

```python
import functools
import math

import jax
import jax.numpy as jnp
from jax import lax
from jax.experimental import pallas as pl
from jax.experimental.pallas import tpu as pltpu

F32 = jnp.float32
BF16 = jnp.bfloat16
HIGHEST = lax.Precision.HIGHEST

D_MODEL = 1024
EPS = 1e-6
GRID_W = 64
TILE = 256
SUBLANES = 8
HALO = SUBLANES
VMEM_LIMIT = 56 * 1024 * 1024

SSM_GROUPS, SSM_GROUP, SSM_STATE = 16, 16, 64
W_SSM = 256
N_STATE = SSM_GROUPS * SSM_STATE
W_LRU, LRU_BLOCKS, LRU_BLOCK, LRU_C = 256, 8, 32, 8.0
DA_HEADS, DA_HEAD, DA_VDIM = 4, 32, 64
N_PAIR = 2 * DA_HEADS
ROPE_BASE = 10000.0
GLA_HEADS, GLA_DK, GLA_DV, GLA_RANK, GLA_TAU, GLA_CHUNK = 4, 32, 64, 16, 16.0, 64
W_GLA_K = GLA_HEADS * GLA_DK
N_BRANCH, W_BRANCH, D_FF = 4, 256, 2816
W_MAIN = 2304
QKV_START, W_QKV = 768, 768
W_GATES = N_BRANCH * D_MODEL
Q_SCALE = DA_HEAD ** -0.5 * math.log2(math.e)
MAX_SHIFT_SPREAD = 100.0
ATT_TQ = 1024
ATT_TK_TILES = 13
ATT_VROWS = 80


def _dot(a, b, precision=None):
    return jnp.dot(a, b, preferred_element_type=F32, precision=precision)


def _dot_nt(a, b, precision=None):
    return lax.dot_general(a, b, (((1,), (1,)), ((), ())), preferred_element_type=F32, precision=precision)


def _params(*sem):
    return pltpu.CompilerParams(dimension_semantics=sem, vmem_limit_bytes=VMEM_LIMIT)


def _const_spec(shape):
    nd = len(shape)
    return pl.BlockSpec(shape, lambda *_: (0,) * nd)


def _rms(x, g):
    return x * lax.rsqrt(jnp.mean(x * x, axis=-1, keepdims=True) + EPS) * g


def _group_mean(sq, grp):
    hi = sq.astype(BF16)
    lo = (sq - hi.astype(F32)).astype(BF16)
    return _dot(hi, grp) + _dot(lo, grp)


def _adaln_kernel(c_ref, w_ref, b_ref, o_ref):
    c = c_ref[...]
    o_ref[0] = _dot(c * jax.nn.sigmoid(c), w_ref[0], HIGHEST) + b_ref[0]


def _adaln(cond, w_ada, b_ada):
    depth, _, n_out = w_ada.shape
    blk = 1536
    return pl.pallas_call(
        _adaln_kernel,
        grid=(depth, n_out // blk),
        in_specs=[pl.BlockSpec((SUBLANES, D_MODEL), lambda l, n: (0, 0)),
                  pl.BlockSpec((1, D_MODEL, blk), lambda l, n: (l, 0, n)),
                  pl.BlockSpec((1, 1, blk), lambda l, n: (l, 0, n))],
        out_specs=pl.BlockSpec((1, SUBLANES, blk), lambda l, n: (l, 0, n)),
        out_shape=jax.ShapeDtypeStruct((depth, SUBLANES, n_out), F32),
        compiler_params=_params("arbitrary", "arbitrary"),
        name="adaln",
    )(cond, w_ada, b_ada.reshape(depth, 1, n_out))


def _inproj_kernel(x_ref, mod_ref, g_ref, wqkv_ref, wm_ref, wa_ref, wg_ref, cos_ref, sin_ref, qg_ref, kg_ref, grp_ref,
                   u_ref, lx_ref, ly_ref, qt_ref, k_ref, vt_ref, gqk_ref, gv_ref, gg_ref, ga_ref, gates_ref,
                   qmax_ref, kmax_ref):
    x = x_ref[...]
    mod = mod_ref[0]
    hn = _rms(x, g_ref[...]) * (1.0 + mod[1:2]) + mod[0:1]
    hb = hn.astype(BF16)
    zq = _dot(hb, wqkv_ref[...])

    cos, sin = cos_ref[...], sin_ref[...]
    lane = lax.broadcasted_iota(jnp.int32, (TILE, 256), 1)
    first_half = (lane % 16) < 8
    grp = grp_ref[...]

    def qk_norm_rope(t, g):
        tn = t * lax.rsqrt(_group_mean(t * t, grp) + EPS) * g
        partner = jnp.where(first_half, pltpu.roll(tn, 256 - 8, 1), pltpu.roll(tn, 8, 1))
        return tn * cos + partner * sin

    def max_sq_norm(tb):
        t = tb.astype(F32)
        return jnp.max(_dot((t * t).astype(BF16), grp) * DA_HEAD, axis=0, keepdims=True)

    @pl.when(pl.program_id(0) == 0)
    def _():
        qmax_ref[...] = jnp.zeros_like(qmax_ref)
        kmax_ref[...] = jnp.zeros_like(kmax_ref)

    qb = (qk_norm_rope(zq[:, 0:256], qg_ref[...]) * Q_SCALE).astype(BF16)
    kb = qk_norm_rope(zq[:, 256:512], kg_ref[...]).astype(BF16)
    qmax_ref[...] = jnp.maximum(qmax_ref[...], max_sq_norm(qb))
    kmax_ref[...] = jnp.maximum(kmax_ref[...], max_sq_norm(kb))
    qt_ref[...] = qb.T
    for p in range(N_PAIR):
        k_ref[p] = kb[:, p * DA_HEAD:(p + 1) * DA_HEAD]
    lane_v = lax.broadcasted_iota(jnp.int32, (TILE, 128), 1)
    tail = (lane_v == DA_VDIM).astype(F32)
    for h in range(DA_HEADS):
        vh = zq[:, 512 + 128 * (h // 2):512 + 128 * (h // 2 + 1)]
        if h % 2:
            vh = pltpu.roll(vh, DA_VDIM, 1)
        vt_ref[h] = jnp.where(lane_v < DA_VDIM, vh, tail).T[:ATT_VROWS].astype(BF16)

    z = _dot(hb, wm_ref[...])
    u_ref[...] = z[:, 0:256]
    lx_ref[...] = z[:, 256:512]
    ly_ref[...] = z[:, 512:768]
    gqk_ref[...] = z[:, 768:1024]
    gv_ref[...] = z[:, 1024:1280]
    gg_ref[...] = z[:, 1280:1536]
    ga_ref[...] = _dot(hb, wa_ref[...])
    gates_ref[...] = _dot(hb, wg_ref[...]).astype(BF16)


def _inproj(xa, mods, norm_g, w_qkv, w_main, w_a, w_gates, cos_t, sin_t, qg, kg, grp32):
    t_all = xa.shape[0]
    nt = t_all // TILE
    row = lambda w: pl.BlockSpec((TILE, w), lambda i: (i, 0))
    f = lambda w: jax.ShapeDtypeStruct((t_all, w), F32)
    return pl.pallas_call(
        _inproj_kernel,
        grid=(nt,),
        in_specs=[row(D_MODEL),
                  pl.BlockSpec((1, SUBLANES, D_MODEL), lambda i: (jnp.minimum(i, 1), 0, 0)),
                  _const_spec((1, D_MODEL)),
                  _const_spec((D_MODEL, W_QKV)), _const_spec((D_MODEL, W_MAIN - W_QKV)), _const_spec((D_MODEL, 128)),
                  _const_spec((D_MODEL, W_GATES)),
                  row(256), row(256), _const_spec((1, 256)), _const_spec((1, 256)), _const_spec((256, 256))],
        out_specs=[row(256), row(256), row(256),
                   pl.BlockSpec((256, TILE), lambda i: (0, i)),
                   pl.BlockSpec((N_PAIR, TILE, DA_HEAD), lambda i: (0, i, 0)),
                   pl.BlockSpec((DA_HEADS, ATT_VROWS, TILE), lambda i: (0, 0, i)),
                   row(256), row(256), row(256), row(128), row(W_GATES),
                   _const_spec((SUBLANES, 256)), _const_spec((SUBLANES, 256))],
        out_shape=[f(256), f(256), f(256), jax.ShapeDtypeStruct((256, t_all), BF16),
                   jax.ShapeDtypeStruct((N_PAIR, t_all, DA_HEAD), BF16),
                   jax.ShapeDtypeStruct((DA_HEADS, ATT_VROWS, t_all), BF16),
                   f(256), f(256), f(256), f(128), jax.ShapeDtypeStruct((t_all, W_GATES), BF16),
                   jax.ShapeDtypeStruct((SUBLANES, 256), F32), jax.ShapeDtypeStruct((SUBLANES, 256), F32)],
        compiler_params=_params("arbitrary"),
        name="inproj",
    )(xa, mods, norm_g, w_qkv, w_main, w_a, w_gates, cos_t, sin_t, qg, kg, grp32)


def _scan_tile_index(j, nt, rev):
    return jnp.where(j == 0, 0, nt - j) if rev else j


DIRECTIONS = (False, True)


def _s5_kernel(uf_ref, ur_ref, b_ref, c_ref, tab_ref, yf_ref, yr_ref, hre, him, car):
    j = pl.program_id(0)

    @pl.when(j == 0)
    def _():
        car[...] = jnp.zeros_like(car)

    for d, u_ref in enumerate((uf_ref, ur_ref)):
        bu = _dot(u_ref[...].astype(BF16), b_ref[d])
        hre[d] = bu[:, :N_STATE]
        him[d] = bu[:, N_STATE:]
    nblk = TILE // SUBLANES

    def body(b, carry):
        out = []
        for d, rev in enumerate(DIRECTIONS):
            cr, ci = carry[d]
            blk = nblk - 1 - b if rev else b
            r0 = pl.multiple_of(blk * SUBLANES, SUBLANES)
            xr = hre[d, pl.ds(r0, SUBLANES), :]
            xi = him[d, pl.ds(r0, SUBLANES), :]
            for k, s in enumerate((1, 2, 4)):
                shift = SUBLANES - s if rev else s
                sr = pltpu.roll(xr, shift, 0)
                si = pltpu.roll(xi, shift, 0)
                mr, mi = tab_ref[d, 2 * k], tab_ref[d, 2 * k + 1]
                xr, xi = xr + mr * sr - mi * si, xi + mr * si + mi * sr
            pr, pi_ = tab_ref[d, 6], tab_ref[d, 7]
            xr, xi = xr + pr * cr - pi_ * ci, xi + pr * ci + pi_ * cr
            hre[d, pl.ds(r0, SUBLANES), :] = xr
            him[d, pl.ds(r0, SUBLANES), :] = xi
            last = 0 if rev else SUBLANES - 1
            out.append((jnp.broadcast_to(xr[last:last + 1], xr.shape), jnp.broadcast_to(xi[last:last + 1], xi.shape)))
        return tuple(out)

    carry = lax.fori_loop(0, nblk, body, tuple((car[d, 0], car[d, 1]) for d in range(2)))
    for d, y_ref in enumerate((yf_ref, yr_ref)):
        car[d, 0], car[d, 1] = carry[d]
        y_ref[...] = (_dot(hre[d].astype(BF16), c_ref[d, :N_STATE]) + _dot(him[d].astype(BF16), c_ref[d, N_STATE:]))


def _dir_specs(shape, nt, index=lambda ti: (ti, 0)):
    return [pl.BlockSpec(shape, functools.partial(lambda j, rev: index(_scan_tile_index(j, nt, rev)), rev=rev))
            for rev in DIRECTIONS]


def _s5_scan(u, bcat, ccat, tabs):
    t_all = u.shape[0]
    nt = t_all // TILE
    return pl.pallas_call(
        _s5_kernel,
        grid=(nt,),
        in_specs=_dir_specs((TILE, W_SSM), nt) + [_const_spec((2, W_SSM, 2 * N_STATE)),
                                                   _const_spec((2, 2 * N_STATE, W_SSM)),
                                                   _const_spec((2, 8, SUBLANES, N_STATE))],
        out_specs=_dir_specs((TILE, W_SSM), nt),
        out_shape=[jax.ShapeDtypeStruct((t_all, W_SSM), F32)] * 2,
        scratch_shapes=[pltpu.VMEM((2, TILE, N_STATE), F32), pltpu.VMEM((2, TILE, N_STATE), F32),
                        pltpu.VMEM((2, 2, SUBLANES, N_STATE), F32)],
        compiler_params=_params("arbitrary"),
        name="s5_scan",
    )(u, u, bcat, ccat, tabs)


def _s5_tables(lam_re, lam_im, log_step, b_re, b_im, c_re, c_im, rev):
    dt = jnp.exp(log_step.astype(F32))[:, None]
    lre, lim = lam_re.astype(F32) * dt, lam_im.astype(F32) * dt
    mag = jnp.exp(lre)
    lb_re, lb_im = mag * jnp.cos(lim), mag * jnp.sin(lim)
    num_re, num_im = lb_re - 1.0, lb_im
    den = lam_re * lam_re + lam_im * lam_im
    k_re = (num_re * lam_re + num_im * lam_im) / den
    k_im = (num_im * lam_re - num_re * lam_im) / den
    bb_re = k_re[..., None] * b_re - k_im[..., None] * b_im
    bb_im = k_re[..., None] * b_im + k_im[..., None] * b_re
    eye = jnp.eye(SSM_GROUPS, dtype=F32)
    blk_in = lambda t: jnp.einsum('gpc,gh->gchp', t, eye).reshape(W_SSM, N_STATE)
    blk_out = lambda t: jnp.einsum('gcp,gh->gphc', t, eye).reshape(N_STATE, W_SSM)
    bcat = jnp.concatenate([blk_in(bb_re), blk_in(bb_im)], axis=1).astype(BF16)
    ccat = jnp.concatenate([blk_out(c_re.astype(F32)), -blk_out(c_im.astype(F32))], axis=0).astype(BF16)

    def power(k):
        k = k[:, None].astype(F32)
        m = jnp.exp(k * lre.reshape(1, -1))
        return m * jnp.cos(k * lim.reshape(1, -1)), m * jnp.sin(k * lim.reshape(1, -1))

    r = jnp.arange(SUBLANES)
    tabs = []
    for s in (1, 2, 4):
        pr, pi_ = power(jnp.full((SUBLANES,), s))
        keep = ((r + s <= SUBLANES - 1) if rev else (r >= s))[:, None]
        tabs += [jnp.where(keep, pr, 0.0), jnp.where(keep, pi_, 0.0)]
    tabs += list(power((SUBLANES - r) if rev else (r + 1)))
    return bcat, ccat, jnp.stack(tabs)


def _lru_kernel(xf_ref, xr_ref, xpf_ref, xpr_ref, xnf_ref, xnr_ref, cw_ref, cb_ref, wr_ref, br_ref, wi_ref, bi_ref,
                lam_ref, hf_ref, hr_ref, ext, a_sc, b_sc, car, *, nt):
    j = pl.program_id(0)

    @pl.when(j == 0)
    def _():
        car[...] = jnp.zeros_like(car)

    for d, (rev, x_ref, xp_ref, xn_ref) in enumerate(zip(DIRECTIONS, (xf_ref, xr_ref), (xpf_ref, xpr_ref),
                                                         (xnf_ref, xnr_ref))):
        ti = _scan_tile_index(j, nt, rev)
        prev_ok = (ti >= 2).astype(F32)
        next_ok = jnp.logical_and(ti != 0, ti != nt - 1).astype(F32)
        ext[d, 0:HALO] = xp_ref[...] * prev_ok
        ext[d, HALO:HALO + TILE] = x_ref[...]
        ext[d, HALO + TILE:] = xn_ref[...] * next_ok
        xc = cb_ref[...] + sum(cw_ref[k:k + 1] * ext[d, pl.ds(HALO - 2 + k, TILE), :] for k in range(4))
        xb = xc.astype(BF16)
        r = jax.nn.sigmoid(_dot(xb, wr_ref[d]) + br_ref[d])
        i = jax.nn.sigmoid(_dot(xb, wi_ref[d]) + bi_ref[d])
        log_a = LRU_C * r * jax.nn.log_sigmoid(lam_ref[d])
        a = jnp.exp(log_a)
        a_sc[d] = a
        b_sc[d] = jnp.sqrt(-jnp.tanh(log_a) * (a * a + 1.0)) * (i * xc)
    nblk = TILE // SUBLANES
    rows = lax.broadcasted_iota(jnp.int32, (SUBLANES, W_LRU), 0)

    def body(b, carry):
        out = []
        for d, rev in enumerate(DIRECTIONS):
            blk = nblk - 1 - b if rev else b
            r0 = pl.multiple_of(blk * SUBLANES, SUBLANES)
            a = a_sc[d, pl.ds(r0, SUBLANES), :]
            x = b_sc[d, pl.ds(r0, SUBLANES), :]
            for s in (1, 2, 4):
                shift = SUBLANES - s if rev else s
                keep = (rows + s <= SUBLANES - 1) if rev else (rows >= s)
                a_s = jnp.where(keep, pltpu.roll(a, shift, 0), 1.0)
                x_s = jnp.where(keep, pltpu.roll(x, shift, 0), 0.0)
                a, x = a * a_s, a * x_s + x
            h = x + a * carry[d]
            b_sc[d, pl.ds(r0, SUBLANES), :] = h
            last = 0 if rev else SUBLANES - 1
            out.append(jnp.broadcast_to(h[last:last + 1], h.shape))
        return tuple(out)

    carry = lax.fori_loop(0, nblk, body, (car[0], car[1]))
    for d, h_ref in enumerate((hf_ref, hr_ref)):
        car[d] = carry[d]
        h_ref[...] = b_sc[d]


def _lru_scan(lx, conv_w, conv_b, wr, br, wi, bi, lam):
    t_all = lx.shape[0]
    nt = t_all // TILE
    per = TILE // HALO
    vec = _const_spec((2, 1, W_LRU))
    return pl.pallas_call(
        functools.partial(_lru_kernel, nt=nt),
        grid=(nt,),
        in_specs=(_dir_specs((TILE, W_LRU), nt)
                  + _dir_specs((HALO, W_LRU), nt, lambda ti: (jnp.maximum(ti * per - 1, 0), 0))
                  + _dir_specs((HALO, W_LRU), nt, lambda ti: (jnp.minimum((ti + 1) * per, nt * per - 1), 0))
                  + [_const_spec((4, W_LRU)), _const_spec((1, W_LRU)), _const_spec((2, W_LRU, W_LRU)), vec,
                     _const_spec((2, W_LRU, W_LRU)), vec, vec]),
        out_specs=_dir_specs((TILE, W_LRU), nt),
        out_shape=[jax.ShapeDtypeStruct((t_all, W_LRU), F32)] * 2,
        scratch_shapes=[pltpu.VMEM((2, TILE + 2 * HALO, W_LRU), F32), pltpu.VMEM((2, TILE, W_LRU), F32),
                        pltpu.VMEM((2, TILE, W_LRU), F32), pltpu.VMEM((2, SUBLANES, W_LRU), F32)],
        compiler_params=_params("arbitrary"),
        name="lru_scan",
    )(lx, lx, lx, lx, lx, lx, conv_w, conv_b, wr, br, wi, bi, lam)


def _block_diag(w):
    n, a, b = w.shape
    return jnp.einsum('nab,nm->namb', w, jnp.eye(n, dtype=w.dtype)).reshape(n * a, n * b)


def _attn_kernel(sc_ref, qt_ref, k_ref, vt_ref, o_ref, m_sc, acc_sc, *, nkv, online):
    j = pl.program_id(1)

    @pl.when(j == 0)
    def _():
        acc_sc[...] = jnp.zeros_like(acc_sc)
        for p in range(N_PAIR):
            if online:
                m_sc[p] = jnp.full(m_sc.shape[1:], -jnp.inf, F32)
            else:
                qf = qt_ref[p * DA_HEAD:(p + 1) * DA_HEAD, :].astype(F32)
                m_sc[p] = jnp.sqrt(jnp.sum(qf * qf, axis=0, keepdims=True)) * sc_ref[0, 1 + p]

    for p in range(N_PAIR):
        s = _dot(k_ref[p], qt_ref[p * DA_HEAD:(p + 1) * DA_HEAD, :])
        if online:
            m_old = m_sc[p]
            m_new = jnp.maximum(m_old, jnp.max(s, axis=0, keepdims=True))
            m_sc[p] = m_new
            acc_sc[p] = jnp.exp2(m_old - m_new) * acc_sc[p] + _dot(vt_ref[p // 2], jnp.exp2(s - m_new).astype(BF16))
        else:
            acc_sc[p] += _dot(vt_ref[p // 2], jnp.exp2(s - m_sc[p]).astype(BF16))

    @pl.when(j == nkv - 1)
    def _():
        lam = sc_ref[0, 0]
        heads = []
        for h in range(DA_HEADS):
            a0, a1 = acc_sc[2 * h], acc_sc[2 * h + 1]
            heads.append(a0[:DA_VDIM] / a0[DA_VDIM:DA_VDIM + 1] - lam * (a1[:DA_VDIM] / a1[DA_VDIM:DA_VDIM + 1]))
        o_ref[...] = jnp.concatenate(heads, axis=0).T


def _attention(scal, qt, k, vt, tq, tk, online):
    n_q = qt.shape[1]
    nq, nkv = n_q // tq, k.shape[1] // tk
    return pl.pallas_call(
        functools.partial(_attn_kernel, nkv=nkv, online=online),
        grid=(nq, nkv),
        in_specs=[pl.BlockSpec(memory_space=pltpu.SMEM),
                  pl.BlockSpec((256, tq), lambda i, j: (0, i)),
                  pl.BlockSpec((N_PAIR, tk, DA_HEAD), lambda i, j: (0, j, 0)),
                  pl.BlockSpec((DA_HEADS, ATT_VROWS, tk), lambda i, j: (0, 0, j))],
        out_specs=pl.BlockSpec((tq, DA_HEADS * DA_VDIM), lambda i, j: (i, 0)),
        out_shape=jax.ShapeDtypeStruct((n_q, DA_HEADS * DA_VDIM), F32),
        scratch_shapes=[pltpu.VMEM((N_PAIR, 1, tq), F32), pltpu.VMEM((N_PAIR, ATT_VROWS, tq), F32)],
        compiler_params=_params("arbitrary", "arbitrary"),
        name="diff_attn_online" if online else "diff_attn",
    )(scal, qt, k, vt)


def _gla_kernel(qkf_ref, qkr_ref, vf_ref, vr_ref, gaf_ref, gar_ref, wa_ref, ba_ref, tri_ref, sel_ref, bd_ref,
                of_ref, or_ref, s_sc):
    j = pl.program_id(0)

    @pl.when(j == 0)
    def _():
        s_sc[...] = jnp.zeros_like(s_sc)

    bd = bd_ref[...]
    stack = GLA_HEADS * GLA_CHUNK
    row_head = lax.broadcasted_iota(jnp.int32, (stack, W_GLA_K), 0) // GLA_CHUNK
    q_head = (lax.broadcasted_iota(jnp.int32, (stack, W_GLA_K), 1) // GLA_DK) == row_head
    ii = lax.broadcasted_iota(jnp.int32, (stack, GLA_CHUNK), 0) % GLA_CHUNK
    jj = lax.broadcasted_iota(jnp.int32, (stack, GLA_CHUNK), 1)
    lane_v = lax.broadcasted_iota(jnp.int32, (GLA_CHUNK, 256), 1) // GLA_DV
    nchunk = TILE // GLA_CHUNK
    states = [s_sc[0], s_sc[1]]
    for d, (rev, qk_ref, v_ref, ga_ref, o_ref) in enumerate(zip(DIRECTIONS, (qkf_ref, qkr_ref), (vf_ref, vr_ref),
                                                                (gaf_ref, gar_ref), (of_ref, or_ref))):
        q = qk_ref[:, :W_GLA_K] * (GLA_DK ** -0.5)
        k = qk_ref[:, W_GLA_K:]
        v = v_ref[...]
        ga = ga_ref[...]
        ga_hi = ga.astype(BF16)
        ga_lo = (ga - ga_hi.astype(F32)).astype(BF16)
        zg = _dot(ga_hi, wa_ref[d, 0]) + _dot(ga_lo, wa_ref[d, 0]) + _dot(ga_hi, wa_ref[d, 1])
        lg = jax.nn.log_sigmoid(zg + ba_ref[d]) / GLA_TAU
        lg_hi = lg.astype(BF16)
        lg_lo = (lg - lg_hi.astype(F32)).astype(BF16)
        b = _dot(tri_ref[d], lg_hi) + _dot(tri_ref[d], lg_lo)
        b_last = _dot(sel_ref[...], lg_hi) + _dot(sel_ref[...], lg_lo)
        qc = (q * jnp.exp(b)).astype(BF16)
        kinv_t = (k * jnp.exp(-b)).T.astype(BF16)
        kdec_t = (k * jnp.exp(b_last - b)).T.astype(BF16)
        decay_t = jnp.exp(b_last).T
        vb = v.astype(BF16)
        causal = (jj >= ii) if rev else (jj <= ii)
        s_state = states[d]
        for ci in range(nchunk):
            c = nchunk - 1 - ci if rev else ci
            rs = slice(c * GLA_CHUNK, (c + 1) * GLA_CHUNK)
            qcc, vc = qc[rs], vb[rs]
            q_stack = jnp.where(q_head, jnp.concatenate([qcc] * GLA_HEADS, axis=0), 0.0)
            att = jnp.where(causal, _dot(q_stack, kinv_t[:, rs]), 0.0)
            o_stack = _dot(att.astype(BF16), vc)
            o = _dot(qcc, s_state.astype(BF16))
            for h in range(GLA_HEADS):
                o = o + jnp.where(lane_v == h, o_stack[h * GLA_CHUNK:(h + 1) * GLA_CHUNK], 0.0)
            o_ref[rs, :] = o
            kv = _dot(kdec_t[:, rs], vc) * bd
            s_state = decay_t[:, c * GLA_CHUNK:c * GLA_CHUNK + 1] * s_state + kv
        states[d] = s_state
    s_sc[0], s_sc[1] = states


def _gla_scan(gqk, gv, ga, wa, ba, tri, sel, bd):
    t_all = gqk.shape[0]
    nt = t_all // TILE
    return pl.pallas_call(
        _gla_kernel,
        grid=(nt,),
        in_specs=(_dir_specs((TILE, 256), nt) + _dir_specs((TILE, 256), nt) + _dir_specs((TILE, 128), nt)
                  + [_const_spec((2, 2, 128, W_GLA_K)), _const_spec((2, 1, W_GLA_K)),
                     _const_spec((2, TILE, TILE)), _const_spec((TILE, TILE)), _const_spec((W_GLA_K, 256))]),
        out_specs=_dir_specs((TILE, 256), nt),
        out_shape=[jax.ShapeDtypeStruct((t_all, 256), F32)] * 2,
        scratch_shapes=[pltpu.VMEM((2, W_GLA_K, 256), F32)],
        compiler_params=_params("arbitrary"),
        name="gla_scan",
    )(gqk, gqk, gv, gv, ga, ga, wa, ba, tri, sel, bd)


def _merge_kernel(x_ref, mod_ref, u_ref, yf_ref, yb_ref, d_ref, wglu_ref, hf_ref, hb_ref, ly_ref,
                  att_ref, dan_ref, gf_ref, gb_ref, gg_ref, gn_ref, grp_ref, gates_ref, wb_ref, wo_ref,
                  o_ref, *, da_scale):
    z = jax.nn.gelu(u_ref[...] * d_ref[...] + yf_ref[...] + yb_ref[...])
    ya = z * jax.nn.sigmoid(_dot(z.astype(BF16), wglu_ref[...]))
    yb = (hf_ref[...] + hb_ref[...]) * jax.nn.gelu(ly_ref[...])
    og = gf_ref[...] + gb_ref[...]
    grp = grp_ref[...]
    gg = gg_ref[...]
    yd = og * lax.rsqrt(_group_mean(og * og, grp) + EPS) * gn_ref[...] * (gg * jax.nn.sigmoid(gg))
    att = att_ref[...]
    yc = att * lax.rsqrt(_group_mean(att * att, grp) + EPS) * (dan_ref[...] * da_scale)

    def gate(n):
        return jax.nn.sigmoid(gates_ref[:, n * D_MODEL:(n + 1) * D_MODEL].astype(F32))

    acc = gate(0) * _dot(ya.astype(BF16), wb_ref[0])
    for n, y in ((1, yb), (2, yc), (3, yd)):
        acc = acc + gate(n) * _dot(y.astype(BF16), wb_ref[n])
    o_ref[...] = x_ref[...] + mod_ref[0, 2:3] * _dot(acc.astype(BF16), wo_ref[...])


def _merge(xa, mods, u, yf, yb, ssm_d, w_glu, hf, hb, ly, att, da_norm, gf, gb, gg, gla_norm, grp64, gates,
           w_branch, w_out, da_scale):
    t_all = xa.shape[0]
    nt = t_all // TILE
    row = lambda w: pl.BlockSpec((TILE, w), lambda i: (i, 0))
    vec = lambda w: _const_spec((1, w))
    return pl.pallas_call(
        functools.partial(_merge_kernel, da_scale=da_scale),
        grid=(nt,),
        in_specs=[row(D_MODEL), pl.BlockSpec((1, SUBLANES, D_MODEL), lambda i: (jnp.minimum(i, 1), 0, 0)),
                  row(256), row(256), row(256), vec(256), _const_spec((256, 256)),
                  row(256), row(256), row(256),
                  row(256), vec(256),
                  row(256), row(256), row(256), vec(256), _const_spec((256, 256)), row(W_GATES),
                  _const_spec((N_BRANCH, W_BRANCH, D_MODEL)), _const_spec((D_MODEL, D_MODEL))],
        out_specs=row(D_MODEL),
        out_shape=jax.ShapeDtypeStruct((t_all, D_MODEL), F32),
        compiler_params=_params("arbitrary"),
        name="merge",
    )(xa, mods, u, yf, yb, ssm_d, w_glu, hf, hb, ly, att, da_norm, gf, gb, gg, gla_norm, grp64, gates,
      w_branch, w_out)


def _ffn_kernel(x_ref, xp_ref, xn_ref, mod_ref, g_ref, wa_ref, wg_ref, cw_ref, cb_ref, wd_ref, o_ref, ext, gt_sc,
                *, nt):
    i = pl.program_id(0)
    mod = mod_ref[0]
    g = g_ref[...]
    norm = lambda t: _rms(t, g) * (1.0 + mod[4:5]) + mod[3:4]
    prev_ok = (i >= 2).astype(F32)
    next_ok = jnp.logical_and(i != 0, i != nt - 1).astype(F32)
    x = x_ref[...]
    hn = norm(x)
    ext[0:HALO] = norm(xp_ref[...]) * prev_ok
    ext[HALO:HALO + TILE] = hn
    ext[HALO + TILE:] = norm(xn_ref[...]) * next_ok
    gt_sc[...] = _dot(ext[...].astype(BF16), wg_ref[...])
    gt = cb_ref[...] + sum(cw_ref[k:k + 1] * gt_sc[pl.ds(HALO - 1 + k, TILE), :] for k in range(3))
    a = _dot(hn.astype(BF16), wa_ref[...])
    y = _dot((jax.nn.gelu(gt) * a).astype(BF16), wd_ref[...])
    o_ref[...] = x + mod[5:6] * y


def _ffn(xa, mods, norm_g, w_up_a, w_up_g, conv_w, conv_b, w_down, latents_only):
    t_all = xa.shape[0]
    nt = t_all // TILE
    per = TILE // HALO
    if latents_only:
        out_rows, out_index = t_all - TILE, lambda i: (jnp.maximum(i - 1, 0), 0)
    else:
        out_rows, out_index = t_all, lambda i: (i, 0)
    return pl.pallas_call(
        functools.partial(_ffn_kernel, nt=nt),
        grid=(nt,),
        in_specs=[pl.BlockSpec((TILE, D_MODEL), lambda i: (i, 0)),
                  pl.BlockSpec((HALO, D_MODEL), lambda i: (jnp.maximum(i * per - 1, 0), 0)),
                  pl.BlockSpec((HALO, D_MODEL), lambda i: (jnp.minimum((i + 1) * per, nt * per - 1), 0)),
                  pl.BlockSpec((1, SUBLANES, D_MODEL), lambda i: (jnp.minimum(i, 1), 0, 0)),
                  _const_spec((1, D_MODEL)), _const_spec((D_MODEL, D_FF)), _const_spec((D_MODEL, D_FF)),
                  _const_spec((3, D_FF)), _const_spec((1, D_FF)), _const_spec((D_FF, D_MODEL))],
        out_specs=pl.BlockSpec((TILE, D_MODEL), out_index),
        out_shape=jax.ShapeDtypeStruct((out_rows, D_MODEL), F32),
        scratch_shapes=[pltpu.VMEM((TILE + 2 * HALO, D_MODEL), F32), pltpu.VMEM((TILE + 2 * HALO, D_FF), F32)],
        compiler_params=_params("arbitrary"),
        name="convffn",
    )(xa, xa, xa, mods, norm_g, w_up_a, w_up_g, conv_w, conv_b, w_down)


def _rope_tables(n_lat, n_ctx):
    pos = jnp.arange(n_lat)
    row, col = (pos // GRID_W).astype(F32), (pos % GRID_W).astype(F32)
    half = DA_HEAD // 2
    inv = 1.0 / (ROPE_BASE ** (jnp.arange(0, half, 2, dtype=F32) / half))
    ang = jnp.stack([row[:, None] * inv, col[:, None] * inv], axis=1)
    cos, sin = jnp.cos(ang), jnp.sin(ang)
    cos32 = jnp.stack([cos, cos], axis=2).reshape(n_lat, DA_HEAD)
    sin32 = jnp.stack([-sin, sin], axis=2).reshape(n_lat, DA_HEAD)
    cos_t = jnp.concatenate([jnp.ones((n_ctx, DA_HEAD), F32), cos32], axis=0)
    sin_t = jnp.concatenate([jnp.zeros((n_ctx, DA_HEAD), F32), sin32], axis=0)
    return jnp.tile(cos_t, (1, N_PAIR)), jnp.tile(sin_t, (1, N_PAIR))


def _group_mean_matrix(width, group):
    idx = jnp.arange(width) // group
    return ((idx[:, None] == idx[None, :]).astype(F32) / group).astype(BF16)


def _gla_chunk_matrices(rev):
    t = jnp.arange(TILE)
    same = (t[:, None] // GLA_CHUNK) == (t[None, :] // GLA_CHUNK)
    order = (t[None, :] >= t[:, None]) if rev else (t[None, :] <= t[:, None])
    return (same & order).astype(BF16), same.astype(BF16)


def kernel(x, c, ctx, c_ctx, w_ada, b_ada, norm1_g, norm2_g, w_in, ssm_lam_re, ssm_lam_im, ssm_log_step, ssm_b_re, ssm_b_im, ssm_c_re, ssm_c_im, ssm_d, ssm_w_glu, lru_conv_w, lru_conv_b, lru_wr, lru_br, lru_wi, lru_bi, lru_lam, da_q_norm, da_k_norm, da_lam, da_out_norm, gla_wa2, gla_ba, gla_out_norm, w_branch, w_out, w_up, ffn_conv_w, ffn_conv_b, w_down):
    depth = w_in.shape[0]
    n_lat, n_ctx = x.shape[1], ctx.shape[1]
    assert x.shape[0] == 1 and n_ctx == TILE and n_lat % TILE == 0
    xa = jnp.concatenate([ctx[0], x[0]], axis=0).astype(F32)

    cond = jnp.zeros((SUBLANES, D_MODEL), F32).at[0].set(c_ctx).at[1].set(c[0])
    ada = _adaln(cond, w_ada, b_ada)
    mods_all = jnp.pad(ada[:, :2].reshape(depth, 2, 6, D_MODEL), ((0, 0), (0, 0), (0, SUBLANES - 6), (0, 0)))

    cos_t, sin_t = _rope_tables(n_lat, n_ctx)
    grp32 = _group_mean_matrix(256, DA_HEAD)
    grp64 = _group_mean_matrix(256, GLA_DV)
    bd_mask = ((jnp.arange(W_GLA_K)[:, None] // GLA_DK) == (jnp.arange(256)[None, :] // GLA_DV)).astype(F32)
    gla_tri = jnp.stack([_gla_chunk_matrices(rev)[0] for rev in DIRECTIONS])
    gla_sel = _gla_chunk_matrices(False)[1]
    tk = TILE * max(d for d in range(1, ATT_TK_TILES + 1) if ((n_ctx + n_lat) // TILE) % d == 0)
    tq = min(ATT_TQ, n_lat)

    for l in range(depth):
        mods = mods_all[l]
        w_l = w_in[l]
        w_qkv = w_l[:, QKV_START:QKV_START + W_QKV].astype(BF16)
        w_main = jnp.concatenate([w_l[:, :QKV_START], w_l[:, QKV_START + W_QKV:W_MAIN]], axis=1).astype(BF16)
        w_a = jnp.pad(w_l[:, W_MAIN:W_MAIN + 2 * GLA_RANK], ((0, 0), (0, 128 - 2 * GLA_RANK))).astype(BF16)
        w_gates = w_l[:, W_MAIN + 2 * GLA_RANK:].astype(BF16)
        qg = jnp.tile(da_q_norm[l], N_PAIR)[None]
        kg = jnp.tile(da_k_norm[l], N_PAIR)[None]
        u, lx, ly, qt, k, vt, gqk, gv, gg, ga, gates, qmax2, kmax2 = _inproj(
            xa, mods, norm1_g[l][None], w_qkv, w_main, w_a, w_gates, cos_t, sin_t, qg, kg, grp32)

        s5_tabs = [_s5_tables(ssm_lam_re[l, di], ssm_lam_im[l, di], ssm_log_step[l, di], ssm_b_re[l, di],
                              ssm_b_im[l, di], ssm_c_re[l, di], ssm_c_im[l, di], rev)
                   for di, rev in enumerate(DIRECTIONS)]
        ys = _s5_scan(u, *(jnp.stack(t) for t in zip(*s5_tabs)))
        hs = _lru_scan(lx, lru_conv_w[l], lru_conv_b[l][None],
                       jnp.stack([_block_diag(lru_wr[l, di]) for di in range(2)]).astype(BF16), lru_br[l][:, None],
                       jnp.stack([_block_diag(lru_wi[l, di]) for di in range(2)]).astype(BF16), lru_bi[l][:, None],
                       lru_lam[l][:, None])
        wa = jnp.stack([jnp.zeros((128, W_GLA_K), F32).at[di * GLA_RANK:(di + 1) * GLA_RANK].set(gla_wa2[l, di])
                        for di in range(2)])
        wa_hi = wa.astype(BF16)
        wa_split = jnp.stack([wa_hi, (wa - wa_hi.astype(F32)).astype(BF16)], axis=1)
        gs = _gla_scan(gqk, gv, ga, wa_split, gla_ba[l][:, None], gla_tri, gla_sel, bd_mask)

        lp = da_lam[l].astype(F32)
        lam_init = 0.8 - 0.6 * math.exp(-0.3 * l)
        lam = jnp.exp(jnp.sum(lp[0] * lp[1])) - jnp.exp(jnp.sum(lp[2] * lp[3])) + lam_init
        qmax = jnp.sqrt(qmax2[0, ::DA_HEAD])
        kmax = jnp.sqrt(kmax2[0, ::DA_HEAD])
        scal = jnp.concatenate([lam.reshape(1), kmax, jnp.zeros((7,), F32)]).reshape(1, 16)

        def attend(online):
            att_lat = _attention(scal, qt[:, n_ctx:], k, vt, tq, tk, online)
            att_ctx = _attention(scal, qt[:, :n_ctx], k[:, :n_ctx], vt[:, :, :n_ctx], n_ctx, n_ctx, online)
            return att_ctx, att_lat

        safe = 2.0 * jnp.max(qmax * kmax) < MAX_SHIFT_SPREAD
        att_ctx, att_lat = lax.cond(safe, lambda: attend(False), lambda: attend(True))
        att = jnp.concatenate([att_ctx, att_lat], axis=0)

        xa = _merge(xa, mods, u, ys[0], ys[1], ssm_d[l][None], ssm_w_glu[l].astype(BF16), hs[0], hs[1], ly,
                    att, jnp.tile(da_out_norm[l], DA_HEADS)[None], gs[0], gs[1], gg,
                    jnp.tile(gla_out_norm[l], GLA_HEADS)[None],
                    grp64, gates, w_branch[l].astype(BF16), w_out[l].astype(BF16), 1.0 - lam_init)
        xa = _ffn(xa, mods, norm2_g[l][None], w_up[l][:, :D_FF].astype(BF16), w_up[l][:, D_FF:].astype(BF16),
                  ffn_conv_w[l], ffn_conv_b[l][None], w_down[l].astype(BF16), latents_only=(l == depth - 1))
    return xa[None].astype(x.dtype)
```
